```python
import math
import jax, jax.numpy as jnp
from jax import lax
import numpy as np

D_MODEL = 1024
BATCH = 4
SEQ = 8192
DEPTH = 2
DEC_BATCH = 8
DEC_SEQ = 64
PAST_LEN = 2048

CHUNK = 64
Q_BLOCK = 128
HEAD_DIM = 64
EPS = 1e-6
NEG_INF = -1e30
SCALE = HEAD_DIM ** -0.5

A_HEADS = D_MODEL // 256
A_WIDTH = A_HEADS * 2 * HEAD_DIM
B_HEADS = D_MODEL // 128
B_WIDTH = B_HEADS * HEAD_DIM
B_PREV_CHUNKS = 8
B_WIN = B_PREV_CHUNKS * CHUNK
B_BAND = B_WIN + CHUNK
B_REL_CLIP = 128
AB_SIZES = (A_WIDTH, A_WIDTH, A_WIDTH, B_WIDTH, B_WIDTH, B_WIDTH)

C_HEADS = D_MODEL // 128
C_WIDTH = C_HEADS * HEAD_DIM
D_WIDTH = D_MODEL // 2
D_BLOCKS = 8
D_BLOCK_DIM = D_WIDTH // D_BLOCKS
D_CONV = 4
RG_C = 8.0
CD_SIZES = (C_WIDTH, C_WIDTH, C_WIDTH, C_HEADS, D_WIDTH, D_WIDTH)

MIX_WIDTH = A_WIDTH + B_WIDTH
FFN_HIDDEN = ((8 * D_MODEL // 3 + 255) // 256) * 256

kernel_name = 'hybrid_stream_encoder_step'

F32 = jnp.float32


def lambda_init(layer):
    return 0.8 - 0.6 * math.exp(-0.3 * layer)


def alibi_slopes(n):
    return jnp.asarray([2.0 ** (-8.0 * (h + 1) / n) for h in range(n)], dtype=F32)


def split_cols(x, sizes):
    cuts = [int(c) for c in np.cumsum(sizes)[:-1]]
    return jnp.split(x, cuts, axis=-1)


def rmsnorm(x, g):
    xf = x.astype(F32)
    y = xf * lax.rsqrt(jnp.mean(xf * xf, axis=-1, keepdims=True) + EPS) * g.astype(F32)
    return y.astype(x.dtype)


def swiglu(h, w1, w3, w2):
    return (jax.nn.silu(h @ w1) * (h @ w3)) @ w2


def sweep_query_blocks(fn, q_arrays, q_pos):
    nb = q_pos.shape[0] // Q_BLOCK
    def split(a):
        return jnp.moveaxis(a.reshape((a.shape[0], nb, Q_BLOCK) + a.shape[2:]), 1, 0)
    blocks = tuple(split(a) for a in q_arrays) + (q_pos.reshape(nb, Q_BLOCK),)
    out = lax.map(lambda args: fn(*args), blocks)
    out = jnp.moveaxis(out, 0, 1)
    return out.reshape((out.shape[0], nb * Q_BLOCK) + out.shape[3:])


def diff_attn_core(q, k, v, q_pos, k_pos, lam):
    dist = jnp.abs(q_pos[:, None] - k_pos[None, :]).astype(F32)
    bias = -alibi_slopes(A_HEADS)[:, None, None] * dist
    mask = (k_pos[None, :] // CHUNK) <= (q_pos[:, None] // CHUNK)
    s = jnp.einsum('bqhcd,bkhcd->bchqk', q, k).astype(F32) * SCALE + bias
    p = jax.nn.softmax(jnp.where(mask, s, NEG_INF), axis=-1)
    w = p[:, 0] - lam * p[:, 1]
    return jnp.einsum('bhqk,bkhe->bqhe', w.astype(v.dtype), v)


def rel_position_bias(table, rel):
    idx = jnp.clip(rel, -B_REL_CLIP, B_REL_CLIP) + B_REL_CLIP
    return table.astype(F32)[:, idx]


def band_core(q, k, v, bias, valid):
    s = jnp.einsum('bqhd,bkhd->bhqk', q, k).astype(F32) * SCALE + bias
    p = jax.nn.softmax(jnp.where(valid, s, NEG_INF), axis=-1)
    return jnp.einsum('bhqk,bkhd->bqhd', p.astype(v.dtype), v)


def band_attn_prompt(q, k, v, rel_table):
    nb, S, H, d = q.shape
    nc = S // CHUNK
    pad = ((0, 0), (B_WIN, 0), (0, 0), (0, 0))
    kp, vp = jnp.pad(k, pad), jnp.pad(v, pad)
    rel = jnp.arange(CHUNK)[:, None] + B_WIN - jnp.arange(B_BAND)[None, :]
    bias = rel_position_bias(rel_table, rel)
    qc = jnp.moveaxis(q.reshape(nb, nc, CHUNK, H, d), 1, 0)
    def one_chunk(args):
        c, qb = args
        start = c * CHUNK
        kb = lax.dynamic_slice_in_dim(kp, start, B_BAND, axis=1)
        vb = lax.dynamic_slice_in_dim(vp, start, B_BAND, axis=1)
        valid = (start - B_WIN + jnp.arange(B_BAND)) >= 0
        return band_core(qb, kb, vb, bias, valid)
    out = lax.map(one_chunk, (jnp.arange(nc), qc))
    return jnp.moveaxis(out, 0, 1).reshape(nb, S, H, d)


def fox_core(q, k, v, f_q, f_k, q_pos, k_pos):
    s = jnp.einsum('bqhd,bkhd->bhqk', q, k).astype(F32) * SCALE
    s = s + jnp.swapaxes(f_q, 1, 2)[..., :, None] - jnp.swapaxes(f_k, 1, 2)[..., None, :]
    mask = k_pos[None, :] <= q_pos[:, None]
    p = jax.nn.softmax(jnp.where(mask, s, NEG_INF), axis=-1)
    return jnp.einsum('bhqk,bkhd->bqhd', p.astype(v.dtype), v)


def causal_conv(u, buf, w, b):
    up = jnp.concatenate([buf.astype(u.dtype), u], axis=1)
    y = lax.conv_general_dilated(up, w[:, None, :].astype(u.dtype), window_strides=(1,), padding='VALID',
                                 dimension_numbers=('NWC', 'WIO', 'NWC'), feature_group_count=u.shape[-1])
    return y + b.astype(u.dtype), up[:, up.shape[1] - (D_CONV - 1):]


def rg_lru(u, h0, w_a, b_a, w_x, b_x, lam):
    nb, T, C = u.shape
    ub = u.reshape(nb, T, D_BLOCKS, D_BLOCK_DIM)
    r = jax.nn.sigmoid(jnp.einsum('btni,nij->btnj', ub, w_a) + b_a).reshape(nb, T, C)
    i = jax.nn.sigmoid(jnp.einsum('btni,nij->btnj', ub, w_x) + b_x).reshape(nb, T, C)
    log_a = -RG_C * r.astype(F32) * jax.nn.softplus(-lam.astype(F32))
    a = jnp.exp(log_a)
    b = jnp.sqrt(-jnp.expm1(2.0 * log_a)) * (i * u).astype(F32)
    b = b.at[:, 0].add(a[:, 0] * h0.astype(F32))
    def combine(left, right):
        a_l, b_l = left
        a_r, b_r = right
        return a_l * a_r, a_r * b_l + b_r
    _, h = lax.associative_scan(combine, (a, b), axis=1)
    return h, h[:, -1]


def ab_mixer(h, w_in, w_out, a_lambda, a_subln_g, b_rel_bias, layer, past):
    nb, T, _ = h.shape
    aq, ak, av, bq, bk, bv = split_cols(h @ w_in, AB_SIZES)
    aq = aq.reshape(nb, T, A_HEADS, 2, HEAD_DIM)
    ak = ak.reshape(nb, T, A_HEADS, 2 * HEAD_DIM)
    av = av.reshape(nb, T, A_HEADS, 2 * HEAD_DIM)
    bq = bq.reshape(nb, T, B_HEADS, HEAD_DIM)
    bk = bk.reshape(nb, T, B_HEADS, HEAD_DIM)
    bv = bv.reshape(nb, T, B_HEADS, HEAD_DIM)
    lam_init = lambda_init(layer)
    lq1, lk1, lq2, lk2 = a_lambda.astype(F32)
    lam = jnp.exp(jnp.sum(lq1 * lk1)) - jnp.exp(jnp.sum(lq2 * lk2)) + lam_init
    if past is None:
        pos = jnp.arange(T, dtype=jnp.int32)
        k_all = ak.reshape(nb, T, A_HEADS, 2, HEAD_DIM)
        o_a = sweep_query_blocks(lambda qb, pb: diff_attn_core(qb, k_all, av, pb, pos, lam), (aq,), pos)
        o_b = band_attn_prompt(bq, bk, bv, b_rel_bias)
        keep = min(B_WIN, T)
        new_state = (ak, av, bk[:, T - keep:], bv[:, T - keep:])
    else:
        cak, cav, cbk, cbv = past
        P, Lb = cak.shape[1], cbk.shape[1]
        q_pos = P + jnp.arange(T, dtype=jnp.int32)
        k_pos = jnp.arange(P + T, dtype=jnp.int32)
        k_all = jnp.concatenate([cak.astype(ak.dtype), ak], axis=1).reshape(nb, P + T, A_HEADS, 2, HEAD_DIM)
        v_all = jnp.concatenate([cav.astype(av.dtype), av], axis=1)
        o_a = diff_attn_core(aq, k_all, v_all, q_pos, k_pos, lam)
        kb = jnp.concatenate([cbk.astype(bk.dtype), bk], axis=1)
        vb = jnp.concatenate([cbv.astype(bv.dtype), bv], axis=1)
        rel = jnp.arange(T)[:, None] + Lb - jnp.arange(Lb + T)[None, :]
        o_b = band_core(bq, kb, vb, rel_position_bias(b_rel_bias, rel), jnp.ones((Lb + T,), dtype=bool))
        new_state = (ak, av, kb[:, T:], vb[:, T:])
    o_a = (rmsnorm(o_a, a_subln_g) * (1.0 - lam_init)).reshape(nb, T, A_WIDTH)
    o_b = o_b.reshape(nb, T, B_WIDTH)
    return jnp.concatenate([o_a, o_b], axis=-1) @ w_out, new_state


def cd_mixer(h, w_in, w_out, c_f_bias, d_conv_w, d_conv_b, d_w_a, d_b_a, d_w_x, d_b_x, d_lambda, past):
    nb, T, _ = h.shape
    cq, ck, cv, cf, dx, dg = split_cols(h @ w_in, CD_SIZES)
    cq = cq.reshape(nb, T, C_HEADS, HEAD_DIM)
    ck = ck.reshape(nb, T, C_HEADS, HEAD_DIM)
    cv = cv.reshape(nb, T, C_HEADS, HEAD_DIM)
    logf = jax.nn.log_sigmoid(cf.astype(F32) + c_f_bias.astype(F32))
    if past is None:
        pos = jnp.arange(T, dtype=jnp.int32)
        F = jnp.cumsum(logf, axis=1)
        o_c = sweep_query_blocks(lambda qb, fb, pb: fox_core(qb, ck, cv, fb, F, pb, pos), (cq, F), pos)
        conv_buf = jnp.zeros((nb, D_CONV - 1, D_WIDTH), dx.dtype)
        h0 = jnp.zeros((nb, D_WIDTH), F32)
    else:
        cck, ccv, cclogf, conv_buf, h0 = past
        P = cck.shape[1]
        F = jnp.cumsum(jnp.concatenate([cclogf.astype(F32), logf], axis=1), axis=1)
        k_all = jnp.concatenate([cck.astype(ck.dtype), ck], axis=1)
        v_all = jnp.concatenate([ccv.astype(cv.dtype), cv], axis=1)
        o_c = fox_core(cq, k_all, v_all, F[:, P:], F, P + jnp.arange(T, dtype=jnp.int32),
                       jnp.arange(P + T, dtype=jnp.int32))
    u, new_buf = causal_conv(dx, conv_buf, d_conv_w, d_conv_b)
    hseq, h_last = rg_lru(u, h0, d_w_a, d_b_a, d_w_x, d_b_x, d_lambda)
    o_d = hseq.astype(dx.dtype) * jax.nn.gelu(dg)
    out = jnp.concatenate([o_c.reshape(nb, T, C_WIDTH), o_d], axis=-1) @ w_out
    return out, (ck, cv, logf, new_buf, h_last.astype(dx.dtype))


def setup_inputs(seed: int = 0) -> dict:
    key = jax.random.key(seed)
    ks = iter(jax.random.split(key, 40))
    def nrm(shape, scale=1.0):
        return jax.random.normal(next(ks), shape, F32) * scale
    b_buf = min(B_WIN, PAST_LEN)
    x_prompt = nrm((BATCH, SEQ, D_MODEL))
    x_sample = nrm((DEC_BATCH, DEC_SEQ, D_MODEL))
    cache_a_k = nrm((DEC_BATCH, PAST_LEN, A_HEADS, 2 * HEAD_DIM))
    cache_a_v = nrm((DEC_BATCH, PAST_LEN, A_HEADS, 2 * HEAD_DIM))
    cache_b_k = nrm((DEC_BATCH, b_buf, B_HEADS, HEAD_DIM))
    cache_b_v = nrm((DEC_BATCH, b_buf, B_HEADS, HEAD_DIM))
    cache_c_k = nrm((DEC_BATCH, PAST_LEN, C_HEADS, HEAD_DIM))
    cache_c_v = nrm((DEC_BATCH, PAST_LEN, C_HEADS, HEAD_DIM))
    cache_c_logf = jax.nn.log_sigmoid(3.0 + nrm((DEC_BATCH, PAST_LEN, C_HEADS)))
    state_d_conv = nrm((DEC_BATCH, D_CONV - 1, D_WIDTH))
    state_d_h = nrm((DEC_BATCH, D_WIDTH), 0.5)
    norm_mix_g = 1.0 + nrm((DEPTH, D_MODEL), 0.02)
    norm_ffn_g = 1.0 + nrm((DEPTH, D_MODEL), 0.02)
    ab_w_in = nrm((D_MODEL, sum(AB_SIZES)), D_MODEL ** -0.5)
    ab_w_out = nrm((MIX_WIDTH, D_MODEL), MIX_WIDTH ** -0.5)
    a_lambda = nrm((4, HEAD_DIM), 0.1)
    a_subln_g = 1.0 + nrm((2 * HEAD_DIM,), 0.02)
    b_rel_bias = nrm((B_HEADS, 2 * B_REL_CLIP + 1), 0.2)
    cd_w_in = nrm((D_MODEL, sum(CD_SIZES)), D_MODEL ** -0.5)
    cd_w_out = nrm((MIX_WIDTH, D_MODEL), MIX_WIDTH ** -0.5)
    c_f_bias = jnp.linspace(1.0, 6.0, C_HEADS, dtype=F32) + nrm((C_HEADS,), 0.1)
    d_conv_w = nrm((D_CONV, D_WIDTH), D_CONV ** -0.5)
    d_conv_b = nrm((D_WIDTH,), 0.01)
    d_w_a = nrm((D_BLOCKS, D_BLOCK_DIM, D_BLOCK_DIM), D_BLOCK_DIM ** -0.5)
    d_b_a = nrm((D_BLOCKS, D_BLOCK_DIM), 0.01)
    d_w_x = nrm((D_BLOCKS, D_BLOCK_DIM, D_BLOCK_DIM), D_BLOCK_DIM ** -0.5)
    d_b_x = nrm((D_BLOCKS, D_BLOCK_DIM), 0.01)
    a_c = jax.random.uniform(next(ks), (D_WIDTH,), F32, 0.9, 0.999) ** (1.0 / RG_C)
    d_lambda = jnp.log(a_c) - jnp.log1p(-a_c)
    ffn_w1 = nrm((DEPTH, D_MODEL, FFN_HIDDEN), D_MODEL ** -0.5)
    ffn_w3 = nrm((DEPTH, D_MODEL, FFN_HIDDEN), D_MODEL ** -0.5)
    ffn_w2 = nrm((DEPTH, FFN_HIDDEN, D_MODEL), FFN_HIDDEN ** -0.5)
    final_g = 1.0 + nrm((D_MODEL,), 0.02)
    return {'x_prompt': x_prompt, 'x_sample': x_sample,
            'cache_a_k': cache_a_k, 'cache_a_v': cache_a_v, 'cache_b_k': cache_b_k, 'cache_b_v': cache_b_v,
            'cache_c_k': cache_c_k, 'cache_c_v': cache_c_v, 'cache_c_logf': cache_c_logf,
            'state_d_conv': state_d_conv, 'state_d_h': state_d_h,
            'norm_mix_g': norm_mix_g, 'norm_ffn_g': norm_ffn_g,
            'ab_w_in': ab_w_in, 'ab_w_out': ab_w_out, 'a_lambda': a_lambda, 'a_subln_g': a_subln_g,
            'b_rel_bias': b_rel_bias, 'cd_w_in': cd_w_in, 'cd_w_out': cd_w_out, 'c_f_bias': c_f_bias,
            'd_conv_w': d_conv_w, 'd_conv_b': d_conv_b, 'd_w_a': d_w_a, 'd_b_a': d_b_a,
            'd_w_x': d_w_x, 'd_b_x': d_b_x, 'd_lambda': d_lambda,
            'ffn_w1': ffn_w1, 'ffn_w3': ffn_w3, 'ffn_w2': ffn_w2, 'final_g': final_g}


def reference(x_prompt, x_sample, cache_a_k, cache_a_v, cache_b_k, cache_b_v, cache_c_k, cache_c_v,
              cache_c_logf, state_d_conv, state_d_h, norm_mix_g, norm_ffn_g, ab_w_in, ab_w_out, a_lambda,
              a_subln_g, b_rel_bias, cd_w_in, cd_w_out, c_f_bias, d_conv_w, d_conv_b, d_w_a, d_b_a,
              d_w_x, d_b_x, d_lambda, ffn_w1, ffn_w3, ffn_w2, final_g):
    yp, ys = x_prompt, x_sample
    for layer in range(DEPTH):
        hp = rmsnorm(yp, norm_mix_g[layer])
        hs = rmsnorm(ys, norm_mix_g[layer])
        if layer % 2 == 0:
            mp, (pa_k, pa_v, pb_k, pb_v) = ab_mixer(hp, ab_w_in, ab_w_out, a_lambda, a_subln_g, b_rel_bias,
                                                    layer, None)
            ms, (sa_k, sa_v, sb_k, sb_v) = ab_mixer(hs, ab_w_in, ab_w_out, a_lambda, a_subln_g, b_rel_bias,
                                                    layer, (cache_a_k, cache_a_v, cache_b_k, cache_b_v))
        else:
            mp, (pc_k, pc_v, pc_logf, pd_conv, pd_h) = cd_mixer(
                hp, cd_w_in, cd_w_out, c_f_bias, d_conv_w, d_conv_b, d_w_a, d_b_a, d_w_x, d_b_x, d_lambda, None)
            ms, (sc_k, sc_v, sc_logf, sd_conv, sd_h) = cd_mixer(
                hs, cd_w_in, cd_w_out, c_f_bias, d_conv_w, d_conv_b, d_w_a, d_b_a, d_w_x, d_b_x, d_lambda,
                (cache_c_k, cache_c_v, cache_c_logf, state_d_conv, state_d_h))
        yp = yp + mp
        ys = ys + ms
        yp = yp + swiglu(rmsnorm(yp, norm_ffn_g[layer]), ffn_w1[layer], ffn_w3[layer], ffn_w2[layer])
        ys = ys + swiglu(rmsnorm(ys, norm_ffn_g[layer]), ffn_w1[layer], ffn_w3[layer], ffn_w2[layer])
    y_prompt = rmsnorm(yp, final_g)
    y_sample = rmsnorm(ys, final_g)
    return (y_prompt, y_sample,
            pa_k, pa_v, pb_k, pb_v, pc_k, pc_v, pc_logf, pd_conv, pd_h,
            sa_k, sa_v, sb_k, sb_v, sc_k, sc_v, sc_logf, sd_conv, sd_h)
```

```python
import functools
import math

import numpy as np
import jax
import jax.numpy as jnp
from jax import lax
from jax.experimental import pallas as pl
from jax.experimental.pallas import tpu as pltpu

F32 = jnp.float32
BF16 = jnp.bfloat16

D_MODEL = 1024
CHUNK = 64
HEAD_DIM = 64
EPS = 1e-6
NEG_INF = -1e30
SCALE = HEAD_DIM ** -0.5
A_HEADS = 4
B_HEADS = 8
B_WIN = 512
B_REL_CLIP = 128
C_HEADS = 8
D_WIDTH = 512
D_CONV = 4
RG_C = 8.0
FFN_HIDDEN = 2816
LANES = 128
MIB = 1024 * 1024


def _lambda_init(layer):
    return 0.8 - 0.6 * math.exp(-0.3 * layer)


def _params(n_axes, vmem_mib=48):
    return pltpu.CompilerParams(dimension_semantics=("arbitrary",) * n_axes,
                                vmem_limit_bytes=vmem_mib * MIB)


def _rms(x, g):
    return x * lax.rsqrt(jnp.mean(x * x, axis=-1, keepdims=True) + EPS) * g


def _sigmoid(x):
    return 1.0 / (1.0 + jnp.exp(-x))


def _expm1(x):
    u = jnp.exp(x)
    return jnp.where(u == 1.0, x, (u - 1.0) * x / jnp.log(jnp.where(u == 1.0, 2.0, u)))


def _softplus(x):
    return jnp.maximum(x, 0.0) + jnp.log1p(jnp.exp(-jnp.abs(x)))


def _norm_proj_kernel(x_ref, g_ref, w_ref, fb_ref, *out_refs, widths, logsig_last):
    h = _rms(x_ref[...], g_ref[...]).astype(BF16)
    off = 0
    for idx, (o_ref, wd) in enumerate(zip(out_refs, widths)):
        y = jnp.dot(h, w_ref[:, off:off + wd], preferred_element_type=F32)
        if logsig_last and idx == len(widths) - 1:
            y = -_softplus(-(y + fb_ref[...]))
        o_ref[...] = y.astype(o_ref.dtype)
        off += wd


def _norm_proj(x, g, w, fb, widths, dtypes, logsig_last, tm):
    n = x.shape[0]
    kern = functools.partial(_norm_proj_kernel, widths=tuple(widths), logsig_last=logsig_last)
    return pl.pallas_call(
        kern,
        grid=(n // tm,),
        in_specs=[pl.BlockSpec((tm, D_MODEL), lambda i: (i, 0)),
                  pl.BlockSpec((1, D_MODEL), lambda i: (0, 0)),
                  pl.BlockSpec(w.shape, lambda i: (0, 0)),
                  pl.BlockSpec((1, LANES), lambda i: (0, 0))],
        out_specs=[pl.BlockSpec((tm, wd), lambda i: (i, 0)) for wd in widths],
        out_shape=[jax.ShapeDtypeStruct((n, wd), dt) for wd, dt in zip(widths, dtypes)],
        compiler_params=_params(1),
        name="norm_proj",
    )(x, g, w, fb)


def _cumsum_kernel(x_ref, o_ref, carry_ref, *, tt):
    @pl.when(pl.program_id(1) == 0)
    def _():
        carry_ref[...] = jnp.zeros(carry_ref.shape, F32)

    x = x_ref[0]
    row = lax.broadcasted_iota(jnp.int32, (tt, tt), 0)
    col = lax.broadcasted_iota(jnp.int32, (tt, tt), 1)
    tri = jnp.where(col <= row, 1.0, 0.0).astype(BF16)
    x1 = x.astype(BF16)
    r1 = x - x1.astype(F32)
    x2 = r1.astype(BF16)
    x3 = (r1 - x2.astype(F32)).astype(BF16)
    c = (jnp.dot(tri, x1, preferred_element_type=F32)
         + jnp.dot(tri, x2, preferred_element_type=F32)
         + jnp.dot(tri, x3, preferred_element_type=F32))
    out = c + carry_ref[0:1, :]
    o_ref[0] = out
    carry_ref[0:1, :] = out[tt - 1:tt, :]


def _cumsum_time(x, tt):
    b, t, _ = x.shape
    return pl.pallas_call(
        functools.partial(_cumsum_kernel, tt=tt),
        grid=(b, t // tt),
        in_specs=[pl.BlockSpec((1, tt, LANES), lambda i, j: (i, j, 0))],
        out_specs=pl.BlockSpec((1, tt, LANES), lambda i, j: (i, j, 0)),
        out_shape=jax.ShapeDtypeStruct(x.shape, F32),
        scratch_shapes=[pltpu.VMEM((8, LANES), F32)],
        compiler_params=_params(2),
        name="cumsum_time",
    )(x)


def _flash_kernel(qi_tab, kj_tab, last_tab, q_ref, k_ref, v_ref, e1_ref, e2_ref, e3_ref, o_ref,
                  qs_ref, m_ref, l_ref, acc_ref, cq_ref, *, mode, tq, tk, q_off, lam_init):
    h = pl.program_id(1)
    p = pl.program_id(2)
    qi = qi_tab[p]
    kj = kj_tab[p]
    q_start = q_off + qi * tq
    k_start = kj * tk
    lo = lax.broadcasted_iota(jnp.int32, (1, LANES), 1) < HEAD_DIM

    @pl.when(kj == 0)
    def _init():
        qsc = q_ref[0] * jnp.asarray(SCALE, BF16)
        zero = jnp.zeros_like(qsc)
        qs_ref[0] = jnp.where(lo, qsc, zero)
        qs_ref[1] = jnp.where(lo, zero, qsc)
        m_ref[...] = jnp.full(m_ref.shape, NEG_INF, F32)
        l_ref[...] = jnp.zeros(l_ref.shape, F32)
        acc_ref[...] = jnp.zeros(acc_ref.shape, F32)
        if mode == "fox":
            fq = e1_ref[0, 0]
            for half in (0, 1):
                cq_ref[half] = fq[:, half:half + 1]
        else:
            qpos = q_start + lax.broadcasted_iota(jnp.int32, (tq, 1), 0)
            cq_ref[0] = e1_ref[h] * qpos.astype(F32)

    def step(masked):
        k = k_ref[0].astype(BF16)
        v = v_ref[0].astype(BF16)
        kpos = k_start + lax.broadcasted_iota(jnp.int32, (1, tk), 1)
        qpos = q_start + lax.broadcasted_iota(jnp.int32, (tq, 1), 0)
        mask = None
        if mode == "diff":
            slope = e1_ref[h]
            if masked:
                bias = -slope * jnp.abs(qpos - kpos).astype(F32)
                mask = (kpos >> 6) <= (qpos >> 6)
            else:
                rowt = slope * kpos.astype(F32)
        else:
            fk2 = e2_ref[0, 0]
            if masked:
                mask = kpos <= qpos
        for half in (0, 1):
            s = lax.dot_general(qs_ref[half], k, (((1,), (1,)), ((), ())), preferred_element_type=F32)
            if mode == "diff":
                s = (s + bias) if masked else ((s + rowt) - cq_ref[0])
            else:
                s = (s + cq_ref[half]) - fk2[half:half + 1, :]
            if masked:
                s = jnp.where(mask, s, NEG_INF)
            m_prev = m_ref[half]
            m_new = jnp.maximum(m_prev, jnp.max(s, axis=1, keepdims=True))
            alpha = jnp.exp(m_prev - m_new)
            pr = jnp.exp(s - m_new)
            l_ref[half] = alpha * l_ref[half] + jnp.sum(pr, axis=1, keepdims=True)
            acc_ref[half] = alpha * acc_ref[half] + jnp.dot(pr.astype(BF16), v, preferred_element_type=F32)
            m_ref[half] = m_new

    interior = (k_start + tk) <= q_start

    @pl.when(interior)
    def _():
        step(False)

    @pl.when(jnp.logical_not(interior))
    def _():
        step(True)

    @pl.when(last_tab[p] == 1)
    def _fin():
        o0 = acc_ref[0] * (1.0 / l_ref[0])
        o1 = acc_ref[1] * (1.0 / l_ref[1])
        if mode == "diff":
            a = e2_ref[...]
            lam = (jnp.exp(jnp.sum(a[0:1] * a[1:2], axis=1, keepdims=True))
                   - jnp.exp(jnp.sum(a[2:3] * a[3:4], axis=1, keepdims=True)) + lam_init)
            o = o0 - lam * o1
            o_ref[0] = (_rms(o, e3_ref[...]) * (1.0 - lam_init)).astype(o_ref.dtype)
        else:
            o_ref[0] = jnp.where(lo, o0, o1).astype(o_ref.dtype)


def _flash(mode, q, k, v, e1, e2, e3, *, tq, tk, q_off, lam_init=0.0):
    b, t_q, _ = q.shape
    n_col = q.shape[2] // LANES
    pairs = [(qi, kj) for qi in range(t_q // tq)
             for kj in range((q_off + (qi + 1) * tq - 1) // tk + 1)]
    qi_tab = np.asarray([pq[0] for pq in pairs], np.int32)
    kj_tab = np.asarray([pq[1] for pq in pairs], np.int32)
    last_tab = np.asarray([1 if (i + 1 == len(pairs) or pairs[i + 1][0] != pairs[i][0]) else 0
                           for i in range(len(pairs))], np.int32)
    if mode == "diff":
        e_specs = [pl.BlockSpec(memory_space=pltpu.SMEM),
                   pl.BlockSpec(e2.shape, lambda bb, hh, pp, qt, kt, lt: (0, 0)),
                   pl.BlockSpec(e3.shape, lambda bb, hh, pp, qt, kt, lt: (0, 0))]
    else:
        e_specs = [pl.BlockSpec((1, 1, tq, 2), lambda bb, hh, pp, qt, kt, lt: (bb, hh, qt[pp], 0)),
                   pl.BlockSpec((1, 1, 2, tk), lambda bb, hh, pp, qt, kt, lt: (bb, hh, 0, kt[pp])),
                   pl.BlockSpec(e3.shape, lambda bb, hh, pp, qt, kt, lt: (0, 0))]
    grid_spec = pltpu.PrefetchScalarGridSpec(
        num_scalar_prefetch=3,
        grid=(b, n_col, len(pairs)),
        in_specs=[pl.BlockSpec((1, tq, LANES), lambda bb, hh, pp, qt, kt, lt: (bb, qt[pp], hh)),
                  pl.BlockSpec((1, tk, LANES), lambda bb, hh, pp, qt, kt, lt: (bb, kt[pp], hh)),
                  pl.BlockSpec((1, tk, LANES), lambda bb, hh, pp, qt, kt, lt: (bb, kt[pp], hh))] + e_specs,
        out_specs=pl.BlockSpec((1, tq, LANES), lambda bb, hh, pp, qt, kt, lt: (bb, qt[pp], hh)),
        scratch_shapes=[pltpu.VMEM((2, tq, LANES), BF16),
                        pltpu.VMEM((2, tq, 1), F32),
                        pltpu.VMEM((2, tq, 1), F32),
                        pltpu.VMEM((2, tq, LANES), F32),
                        pltpu.VMEM((2, tq, 1), F32)],
    )
    kern = functools.partial(_flash_kernel, mode=mode, tq=tq, tk=tk, q_off=q_off, lam_init=lam_init)
    return pl.pallas_call(
        kern,
        grid_spec=grid_spec,
        out_shape=jax.ShapeDtypeStruct(q.shape, BF16),
        compiler_params=_params(3),
        name="flash_" + mode,
    )(qi_tab, kj_tab, last_tab, q, k, v, e1, e2, e3)


def _band_kernel(q_ref, *rest, nblk, rows, thr0):
    k_refs = rest[:nblk]
    v_refs = rest[nblk:2 * nblk]
    bias_ref = rest[2 * nblk]
    o_ref = rest[2 * nblk + 1]
    c = pl.program_id(1)
    lk = nblk * rows
    valid = lax.broadcasted_iota(jnp.int32, (1, lk), 1) >= (thr0 - c * rows)
    lo = lax.broadcasted_iota(jnp.int32, (1, LANES), 1) < HEAD_DIM
    for pair in range(B_HEADS // 2):
        sl = slice(LANES * pair, LANES * pair + LANES)
        q = q_ref[0, :, sl] * jnp.asarray(SCALE, BF16)
        zero = jnp.zeros_like(q)
        k = jnp.concatenate([r[0, :, sl] for r in k_refs], axis=0).astype(BF16)
        v = jnp.concatenate([r[0, :, sl] for r in v_refs], axis=0).astype(BF16)
        outs = []
        for half in (0, 1):
            qh = jnp.where(lo, q, zero) if half == 0 else jnp.where(lo, zero, q)
            s = lax.dot_general(qh, k, (((1,), (1,)), ((), ())), preferred_element_type=F32)
            s = jnp.where(valid, s + bias_ref[2 * pair + half], NEG_INF)
            m = jnp.max(s, axis=1, keepdims=True)
            pr = jnp.exp(s - m)
            den = jnp.sum(pr, axis=1, keepdims=True)
            outs.append(jnp.dot(pr.astype(BF16), v, preferred_element_type=F32) * (1.0 / den))
        o_ref[0, :, sl] = jnp.where(lo, outs[0], outs[1]).astype(o_ref.dtype)


def _band_keys(rows):
    return -(-(B_WIN + rows) // LANES) * LANES


def _band_bias(table, rows):
    lk = _band_keys(rows)
    i = np.arange(rows)[:, None]
    j = np.arange(lk)[None, :]
    idx = np.clip(i + B_WIN - j, -B_REL_CLIP, B_REL_CLIP) + B_REL_CLIP
    lo_edge = (i // CHUNK) * CHUNK
    inband = (j >= lo_edge) & (j < lo_edge + B_WIN + CHUNK)
    bias = table.astype(F32)[:, idx]
    return jnp.where(jnp.asarray(inband)[None], bias, NEG_INF)


def _band(q, kpad, vpad, table, *, rows, off):
    b, t, w = q.shape
    nblk = _band_keys(rows) // rows
    need = (t // rows - 1 + nblk) * rows - kpad.shape[1]
    if need > 0:
        kpad, vpad = _pad_time(kpad, 0, need), _pad_time(vpad, 0, need)
    bias = _band_bias(table, rows)
    kv_specs = [pl.BlockSpec((1, rows, w), lambda bb, cc, i=i: (bb, cc + i, 0)) for i in range(nblk)]
    kern = functools.partial(_band_kernel, nblk=nblk, rows=rows, thr0=B_WIN - off)
    return pl.pallas_call(
        kern,
        grid=(b, t // rows),
        in_specs=[pl.BlockSpec((1, rows, w), lambda bb, cc: (bb, cc, 0))] + kv_specs + kv_specs
                 + [pl.BlockSpec(bias.shape, lambda bb, cc: (0, 0, 0))],
        out_specs=pl.BlockSpec((1, rows, w), lambda bb, cc: (bb, cc, 0)),
        out_shape=jax.ShapeDtypeStruct(q.shape, BF16),
        compiler_params=_params(2),
        name="band_attn",
    )(q, *([kpad] * nblk), *([vpad] * nblk), bias)


def _gelu_tanh(x):
    return 0.5 * x * (1.0 + jnp.tanh(math.sqrt(2.0 / math.pi) * (x + 0.044715 * (x * x * x))))


def _rglru_kernel(dx_ref, dg_ref, buf_ref, h0_ref, cw_ref, cb_ref, wa_ref, ba_ref, wx_ref, bx_ref, lam_ref,
                  od_ref, nbuf_ref, hl_ref, tail_ref, hst_ref, a_s, b_s, h_s, *, tt):
    j = pl.program_id(1)
    nj = pl.num_programs(1)

    @pl.when(j == 0)
    def _():
        tail_ref[...] = jnp.zeros(tail_ref.shape, F32)
        tail_ref[5:8, :] = buf_ref[0]
        hst_ref[...] = h0_ref[0]

    x = dx_ref[0]
    xs = jnp.concatenate([tail_ref[...], x], axis=0)
    cw = cw_ref[...]
    u = cb_ref[...] + cw[3:4] * x
    for tap in range(D_CONV - 1):
        u = u + cw[tap:tap + 1] * xs[5 + tap:5 + tap + tt]
    tail_ref[...] = x[tt - 8:tt]

    ub = u.astype(BF16)
    r = _sigmoid(jnp.dot(ub, wa_ref[...], preferred_element_type=F32) + ba_ref[...])
    gate_i = _sigmoid(jnp.dot(ub, wx_ref[...], preferred_element_type=F32) + bx_ref[...])
    log_a = -RG_C * r * _softplus(-lam_ref[...])
    a_s[...] = jnp.exp(log_a)
    b_s[...] = jnp.sqrt(-_expm1(2.0 * log_a)) * (gate_i * u)

    def body(i, h):
        base = pl.multiple_of(i * 8, 8)
        for rr in range(8):
            h = a_s[pl.ds(base + rr, 1), :] * h + b_s[pl.ds(base + rr, 1), :]
            h_s[pl.ds(base + rr, 1), :] = h
        return h

    h_fin = lax.fori_loop(0, tt // 8, body, hst_ref[...])
    hst_ref[...] = h_fin
    od_ref[0] = (h_s[...] * _gelu_tanh(dg_ref[0])).astype(od_ref.dtype)

    @pl.when(j == nj - 1)
    def _():
        nbuf_ref[0] = tail_ref[8 - (D_CONV - 1):8, :]
        hl_ref[0] = h_fin


def _rglru(dx, dg, buf, h0, cw, cb, wa, ba, wx, bx, lam, *, tt):
    b, t, w = dx.shape
    const2 = lambda shape: pl.BlockSpec(shape, lambda bb, jj: (0, 0))
    return pl.pallas_call(
        functools.partial(_rglru_kernel, tt=tt),
        grid=(b, t // tt),
        in_specs=[pl.BlockSpec((1, tt, w), lambda bb, jj: (bb, jj, 0)),
                  pl.BlockSpec((1, tt, w), lambda bb, jj: (bb, jj, 0)),
                  pl.BlockSpec((1, D_CONV - 1, w), lambda bb, jj: (bb, 0, 0)),
                  pl.BlockSpec((1, 1, w), lambda bb, jj: (bb, 0, 0)),
                  const2((D_CONV, w)), const2((1, w)), const2((w, w)), const2((1, w)),
                  const2((w, w)), const2((1, w)), const2((1, w))],
        out_specs=[pl.BlockSpec((1, tt, w), lambda bb, jj: (bb, jj, 0)),
                   pl.BlockSpec((1, D_CONV - 1, w), lambda bb, jj: (bb, 0, 0)),
                   pl.BlockSpec((1, 1, w), lambda bb, jj: (bb, 0, 0))],
        out_shape=[jax.ShapeDtypeStruct((b, t, w), BF16),
                   jax.ShapeDtypeStruct((b, D_CONV - 1, w), F32),
                   jax.ShapeDtypeStruct((b, 1, w), F32)],
        scratch_shapes=[pltpu.VMEM((8, w), F32), pltpu.VMEM((1, w), F32),
                        pltpu.VMEM((tt, w), F32), pltpu.VMEM((tt, w), F32), pltpu.VMEM((tt, w), F32)],
        compiler_params=_params(2),
        name="rglru",
    )(dx, dg, buf, h0, cw, cb, wa, ba, wx, bx, lam)


def _tail_kernel(x_ref, o1_ref, o2_ref, wo_ref, g_ref, w1_ref, w3_ref, w2_ref, fg_ref, out_ref,
                 y_ref, h_ref, acc_ref, *, final_norm):
    j = pl.program_id(1)
    half = wo_ref.shape[0] // 2

    @pl.when(j == 0)
    def _():
        y = (x_ref[...]
             + jnp.dot(o1_ref[...], wo_ref[0:half, :], preferred_element_type=F32)
             + jnp.dot(o2_ref[...], wo_ref[half:, :], preferred_element_type=F32))
        y_ref[...] = y
        h_ref[...] = _rms(y, g_ref[...]).astype(BF16)
        acc_ref[...] = jnp.zeros(acc_ref.shape, F32)

    h = h_ref[...]
    a = jnp.dot(h, w1_ref[...], preferred_element_type=F32)
    g = jnp.dot(h, w3_ref[...], preferred_element_type=F32)
    act = (a * _sigmoid(a) * g).astype(BF16)
    acc_ref[...] += jnp.dot(act, w2_ref[...], preferred_element_type=F32)

    @pl.when(j == pl.num_programs(1) - 1)
    def _():
        y = y_ref[...] + acc_ref[...]
        if final_norm:
            y = _rms(y, fg_ref[...])
        out_ref[...] = y


def _tail(x, o1, o2, wo, g, w1, w3, w2, fg, *, final_norm, tm, hc):
    n = x.shape[0]
    hw = o1.shape[1]
    return pl.pallas_call(
        functools.partial(_tail_kernel, final_norm=final_norm),
        grid=(n // tm, FFN_HIDDEN // hc),
        in_specs=[pl.BlockSpec((tm, D_MODEL), lambda i, j: (i, 0)),
                  pl.BlockSpec((tm, hw), lambda i, j: (i, 0)),
                  pl.BlockSpec((tm, hw), lambda i, j: (i, 0)),
                  pl.BlockSpec(wo.shape, lambda i, j: (0, 0)),
                  pl.BlockSpec((1, D_MODEL), lambda i, j: (0, 0)),
                  pl.BlockSpec((D_MODEL, hc), lambda i, j: (0, j)),
                  pl.BlockSpec((D_MODEL, hc), lambda i, j: (0, j)),
                  pl.BlockSpec((hc, D_MODEL), lambda i, j: (j, 0)),
                  pl.BlockSpec((1, D_MODEL), lambda i, j: (0, 0))],
        out_specs=pl.BlockSpec((tm, D_MODEL), lambda i, j: (i, 0)),
        out_shape=jax.ShapeDtypeStruct((n, D_MODEL), F32),
        scratch_shapes=[pltpu.VMEM((tm, D_MODEL), F32), pltpu.VMEM((tm, D_MODEL), BF16),
                        pltpu.VMEM((tm, D_MODEL), F32)],
        compiler_params=_params(2, 56),
        name="tail_ffn",
    )(x, o1, o2, wo, g, w1, w3, w2, fg)


def _pad_time(x, front, back):
    return jnp.pad(x, ((0, 0), (front, back), (0, 0)))


def _block_diag(w):
    n, d, _ = w.shape
    out = jnp.zeros((n * d, n * d), w.dtype)
    for i in range(n):
        out = out.at[i * d:(i + 1) * d, i * d:(i + 1) * d].set(w[i])
    return out


def _layer_ab(x, past, wts, *, tm, tq, tk, rows):
    b, t, _ = x.shape
    n = b * t
    w_in, slopes, a_lambda, a_subln_g, b_rel_bias, g_mix = wts
    aq, ak, av, bq, bk, bv = _norm_proj(
        x.reshape(n, D_MODEL), g_mix, w_in, jnp.zeros((1, LANES), F32),
        [512] * 6, [BF16, F32, F32, BF16, F32, F32], False, tm)
    r3 = lambda z: z.reshape(b, t, 512)
    aq, ak, av, bq, bk, bv = map(r3, (aq, ak, av, bq, bk, bv))
    if past is None:
        ka, va, q_off = ak, av, 0
        kb, vb, off = _pad_time(bk, B_WIN, 0), _pad_time(bv, B_WIN, 0), 0
        keep = min(B_WIN, t)
        nbk, nbv = bk[:, t - keep:], bv[:, t - keep:]
    else:
        cak, cav, cbk, cbv = past
        p_len = cak.shape[1]
        t_all = p_len + t
        back = (-t_all) % tk
        ka = _pad_time(jnp.concatenate([cak.reshape(b, p_len, 512), ak], axis=1), 0, back)
        va = _pad_time(jnp.concatenate([cav.reshape(b, p_len, 512), av], axis=1), 0, back)
        q_off = p_len
        lb = cbk.shape[1]
        kb = jnp.concatenate([cbk.reshape(b, lb, 512), bk], axis=1)
        vb = jnp.concatenate([cbv.reshape(b, lb, 512), bv], axis=1)
        off = B_WIN
        nbk, nbv = kb[:, t:], vb[:, t:]
    o_a = _flash("diff", aq, ka, va, slopes, a_lambda, a_subln_g.reshape(1, LANES),
                 tq=tq, tk=tk, q_off=q_off, lam_init=_lambda_init(0))
    o_b = _band(bq, kb, vb, b_rel_bias, rows=rows, off=off)
    state = (ak.reshape(b, t, A_HEADS, 2 * HEAD_DIM), av.reshape(b, t, A_HEADS, 2 * HEAD_DIM),
             nbk.reshape(b, -1, B_HEADS, HEAD_DIM), nbv.reshape(b, -1, B_HEADS, HEAD_DIM))
    return o_a.reshape(n, 512), o_b.reshape(n, 512), state


def _layer_cd(x, past, wts, *, tm, tq, tk, tt, tc):
    b, t, _ = x.shape
    n = b * t
    w_in, fb, cw, cb, wa, ba, wx, bx, lam, g_mix = wts
    cq, ck, cv, dx, dg, logf = _norm_proj(
        x.reshape(n, D_MODEL), g_mix, w_in, fb,
        [512] * 5 + [LANES], [BF16, F32, F32, F32, F32, F32], True, tm)
    r3 = lambda z: z.reshape(b, t, z.shape[-1])
    cq, ck, cv, dx, dg, logf = map(r3, (cq, ck, cv, dx, dg, logf))
    if past is None:
        kc, vc, q_off = ck, cv, 0
        f_all = _cumsum_time(logf, tc)
        buf = jnp.zeros((b, D_CONV - 1, D_WIDTH), F32)
        h0 = jnp.zeros((b, 1, D_WIDTH), F32)
    else:
        cck, ccv, cclogf, buf, h0 = past
        p_len = cck.shape[1]
        t_all = p_len + t
        back = (-t_all) % tk
        kc = _pad_time(jnp.concatenate([cck.reshape(b, p_len, 512), ck], axis=1), 0, back)
        vc = _pad_time(jnp.concatenate([ccv.reshape(b, p_len, 512), cv], axis=1), 0, back)
        q_off = p_len
        lf_all = jnp.concatenate([jnp.pad(cclogf.astype(F32), ((0, 0), (0, 0), (0, LANES - C_HEADS))), logf],
                                 axis=1)
        f_all = _cumsum_time(_pad_time(lf_all, 0, back), tc)
        h0 = h0.reshape(b, 1, D_WIDTH)
    f_q = jnp.transpose(f_all[:, q_off:q_off + t, :C_HEADS].reshape(b, t, C_HEADS // 2, 2), (0, 2, 1, 3))
    f_kt = jnp.transpose(f_all[:, :, :C_HEADS].reshape(b, -1, C_HEADS // 2, 2), (0, 2, 3, 1))
    o_c = _flash("fox", cq, kc, vc, f_q, f_kt, jnp.zeros((1, LANES), F32), tq=tq, tk=tk, q_off=q_off)
    o_d, nbuf, hl = _rglru(dx, dg, buf, h0, cw, cb, wa, ba, wx, bx, lam, tt=tt)
    state = (ck.reshape(b, t, C_HEADS, HEAD_DIM), cv.reshape(b, t, C_HEADS, HEAD_DIM),
             logf[:, :, :C_HEADS], nbuf, hl.reshape(b, D_WIDTH))
    return o_c.reshape(n, 512), o_d.reshape(n, 512), state


def kernel(x_prompt, x_sample, cache_a_k, cache_a_v, cache_b_k, cache_b_v, cache_c_k, cache_c_v, cache_c_logf, state_d_conv, state_d_h, norm_mix_g, norm_ffn_g, ab_w_in, ab_w_out, a_lambda, a_subln_g, b_rel_bias, cd_w_in, cd_w_out, c_f_bias, d_conv_w, d_conv_b, d_w_a, d_b_a, d_w_x, d_b_x, d_lambda, ffn_w1, ffn_w3, ffn_w2, final_g):
    bp, sp, _ = x_prompt.shape
    bs, ss, _ = x_sample.shape
    slopes = jnp.asarray([2.0 ** (-8.0 * (h + 1) / A_HEADS) for h in range(A_HEADS)], F32)
    row = lambda z: z.reshape(1, -1).astype(F32)

    ab_wts = (ab_w_in.astype(BF16), slopes, a_lambda.astype(F32), a_subln_g.astype(F32), b_rel_bias,
              row(norm_mix_g[0]))
    w_cd = jnp.concatenate([cd_w_in[:, :1536], cd_w_in[:, 1544:], cd_w_in[:, 1536:1544],
                            jnp.zeros((D_MODEL, LANES - C_HEADS), cd_w_in.dtype)], axis=1).astype(BF16)
    fb = jnp.pad(c_f_bias.astype(F32), (0, LANES - C_HEADS)).reshape(1, LANES)
    cd_wts = (w_cd, fb, d_conv_w.astype(F32), row(d_conv_b), _block_diag(d_w_a).astype(BF16), row(d_b_a),
              _block_diag(d_w_x).astype(BF16), row(d_b_x), row(d_lambda), row(norm_mix_g[1]))
    wo = (ab_w_out.astype(BF16), cd_w_out.astype(BF16))
    w1, w3, w2 = ffn_w1.astype(BF16), ffn_w3.astype(BF16), ffn_w2.astype(BF16)
    fg = row(final_g)

    def trunk(x, past_ab, past_cd, cfg):
        b, t, _ = x.shape
        o_a, o_b, st_ab = _layer_ab(x, past_ab, ab_wts, tm=cfg["tm"], tq=cfg["tq"], tk=cfg["tk"],
                                    rows=cfg["rows"])
        y = _tail(x.reshape(b * t, D_MODEL), o_a, o_b, wo[0], row(norm_ffn_g[0]), w1[0], w3[0], w2[0], fg,
                  final_norm=False, tm=cfg["tm"], hc=cfg["hc"])
        o_c, o_d, st_cd = _layer_cd(y.reshape(b, t, D_MODEL), past_cd, cd_wts, tm=cfg["tm"], tq=cfg["tq"],
                                    tk=cfg["tk"], tt=cfg["tt"], tc=cfg["tc"])
        y = _tail(y, o_c, o_d, wo[1], row(norm_ffn_g[1]), w1[1], w3[1], w2[1], fg,
                  final_norm=True, tm=cfg["tm"], hc=cfg["hc"])
        return (y.reshape(b, t, D_MODEL),) + st_ab + st_cd

    prompt_cfg = dict(tm=512, tq=512, tk=512, rows=256, hc=1408, tt=512, tc=512)
    sample_cfg = dict(tm=512, tq=64, tk=512, rows=64, hc=1408, tt=64, tc=512)
    outp = trunk(x_prompt, None, None, prompt_cfg)
    outs = trunk(x_sample, (cache_a_k, cache_a_v, cache_b_k, cache_b_v),
                 (cache_c_k, cache_c_v, cache_c_logf, state_d_conv, state_d_h), sample_cfg)
    return (outp[0], outs[0]) + outp[1:] + outs[1:]
```

```python
import functools
import math

import numpy as np
import jax
import jax.numpy as jnp
from jax import lax
from jax.experimental import pallas as pl
from jax.experimental.pallas import tpu as pltpu

F32 = jnp.float32
BF16 = jnp.bfloat16

D_MODEL = 1024
CHUNK = 64
HEAD_DIM = 64
EPS = 1e-6
NEG_INF = -1e30
SCALE = HEAD_DIM ** -0.5
A_HEADS = 4
B_HEADS = 8
B_WIN = 512
B_REL_CLIP = 128
C_HEADS = 8
D_WIDTH = 512
D_CONV = 4
RG_C = 8.0
FFN_HIDDEN = 2816
LANES = 128
BF16_ROWS = 16
MIB = 1024 * 1024

EXT_ONES_K = 6
EXT_USED = 9


def _lambda_init(layer):
    return 0.8 - 0.6 * math.exp(-0.3 * layer)


def _params(n_axes, vmem_mib=48):
    return pltpu.CompilerParams(dimension_semantics=("arbitrary",) * n_axes,
                                vmem_limit_bytes=vmem_mib * MIB)


def _rms(x, g):
    return x * lax.rsqrt(jnp.mean(x * x, axis=-1, keepdims=True) + EPS) * g


def _sigmoid(x):
    return 1.0 / (1.0 + jnp.exp(-x))


def _expm1(x):
    u = jnp.exp(x)
    return jnp.where(u == 1.0, x, (u - 1.0) * x / jnp.log(jnp.where(u == 1.0, 2.0, u)))


def _softplus(x):
    return jnp.maximum(x, 0.0) + jnp.log1p(jnp.exp(-jnp.abs(x)))


def _split3(x):
    x1 = x.astype(BF16)
    r1 = x - x1.astype(F32)
    x2 = r1.astype(BF16)
    x3 = (r1 - x2.astype(F32)).astype(BF16)
    return x1, x2, x3


def _norm_proj_kernel(x_ref, g_ref, w_ref, fb_ref, *out_refs, widths, scales, logsig_last):
    h = _rms(x_ref[...], g_ref[...]).astype(BF16)
    off = 0
    for idx, (o_ref, wd) in enumerate(zip(out_refs, widths)):
        y = jnp.dot(h, w_ref[:, off:off + wd], preferred_element_type=F32)
        if logsig_last and idx == len(widths) - 1:
            y = -_softplus(-(y + fb_ref[...]))
        if scales[idx] != 1.0:
            y = y * scales[idx]
        o_ref[...] = y.astype(o_ref.dtype)
        off += wd


def _norm_proj(x, g, w, fb, widths, dtypes, scales, logsig_last, tm):
    n = x.shape[0]
    kern = functools.partial(_norm_proj_kernel, widths=tuple(widths), scales=tuple(scales),
                             logsig_last=logsig_last)
    return pl.pallas_call(
        kern,
        grid=(n // tm,),
        in_specs=[pl.BlockSpec((tm, D_MODEL), lambda i: (i, 0)),
                  pl.BlockSpec((1, D_MODEL), lambda i: (0, 0)),
                  pl.BlockSpec(w.shape, lambda i: (0, 0)),
                  pl.BlockSpec((1, LANES), lambda i: (0, 0))],
        out_specs=[pl.BlockSpec((tm, wd), lambda i: (i, 0)) for wd in widths],
        out_shape=[jax.ShapeDtypeStruct((n, wd), dt) for wd, dt in zip(widths, dtypes)],
        compiler_params=_params(1),
        name="norm_proj",
    )(x, g, w, fb)


def _fox_prep_kernel(x_ref, sel_ref, ones_ref, kx_ref, ft_ref, carry_ref, *, tt):
    @pl.when(pl.program_id(1) == 0)
    def _():
        carry_ref[...] = jnp.zeros(carry_ref.shape, F32)

    lane = lax.broadcasted_iota(jnp.int32, (1, LANES), 1)
    x = jnp.where(lane < C_HEADS, x_ref[0], 0.0)
    row = lax.broadcasted_iota(jnp.int32, (tt, tt), 0)
    col = lax.broadcasted_iota(jnp.int32, (tt, tt), 1)
    tri = jnp.where(col <= row, 1.0, 0.0).astype(BF16)
    x1, x2, x3 = _split3(x)
    f = (jnp.dot(tri, x1, preferred_element_type=F32) + jnp.dot(tri, x2, preferred_element_type=F32)
         + jnp.dot(tri, x3, preferred_element_type=F32)) + carry_ref[0:1, :]
    carry_ref[0:1, :] = f[tt - 1:tt, :]
    f1, f2, f3 = _split3(f)
    pieces = jnp.concatenate([f1, f2, f3], axis=1)
    kx_ref[0] = (ones_ref[...] - jnp.dot(pieces, sel_ref[...], preferred_element_type=F32)).astype(BF16)
    r = lax.broadcasted_iota(jnp.int32, (BF16_ROWS, LANES), 0)
    c = lax.broadcasted_iota(jnp.int32, (BF16_ROWS, LANES), 1)
    eye = jnp.where(r == c, 1.0, 0.0).astype(BF16)
    dn = (((1,), (1,)), ((), ()))
    ft = (lax.dot_general(eye, f1, dn, preferred_element_type=F32)
          + lax.dot_general(eye, f2, dn, preferred_element_type=F32)
          + lax.dot_general(eye, f3, dn, preferred_element_type=F32))
    ft_ref[0] = ft[0:C_HEADS, :]


def _fox_sel():
    sel = np.zeros((3 * LANES, C_HEADS // 2 * LANES), np.float32)
    ones = np.zeros((1, C_HEADS // 2 * LANES), np.float32)
    for pair in range(C_HEADS // 2):
        for half in (0, 1):
            for piece in range(3):
                sel[piece * LANES + 2 * pair + half, pair * LANES + 3 * half + piece] = 1.0
        ones[0, pair * LANES + EXT_ONES_K:pair * LANES + EXT_USED] = 1.0
    return jnp.asarray(sel, BF16), jnp.asarray(ones, F32)


def _fox_prep(logf, tt):
    b, t, _ = logf.shape
    sel, ones = _fox_sel()
    return pl.pallas_call(
        functools.partial(_fox_prep_kernel, tt=tt),
        grid=(b, t // tt),
        in_specs=[pl.BlockSpec((1, tt, LANES), lambda i, j: (i, j, 0)),
                  pl.BlockSpec(sel.shape, lambda i, j: (0, 0)),
                  pl.BlockSpec(ones.shape, lambda i, j: (0, 0))],
        out_specs=[pl.BlockSpec((1, tt, sel.shape[1]), lambda i, j: (i, j, 0)),
                   pl.BlockSpec((1, C_HEADS, tt), lambda i, j: (i, 0, j))],
        out_shape=[jax.ShapeDtypeStruct((b, t, sel.shape[1]), BF16),
                   jax.ShapeDtypeStruct((b, C_HEADS, t), F32)],
        scratch_shapes=[pltpu.VMEM((8, LANES), F32)],
        compiler_params=_params(2),
        name="fox_prep",
    )(logf, sel, ones)


def _flash_kernel(qi_tab, kj_tab, last_tab, q_ref, k_ref, kx_ref, vt_ref, e1_ref, e2_ref, e3_ref, o_ref,
                  qt_ref, m_ref, acc_ref, *, mode, tq, tk, q_off, lam_init):
    h = pl.program_id(1)
    p = pl.program_id(2)
    qi = qi_tab[p]
    kj = kj_tab[p]
    q_start = q_off + qi * tq
    k_start = kj * tk
    row_lo = lax.broadcasted_iota(jnp.int32, (LANES, 1), 0) < HEAD_DIM

    @pl.when(kj == 0)
    def _init():
        qt = q_ref[0].astype(F32).T
        zero = jnp.zeros_like(qt)
        r16 = lax.broadcasted_iota(jnp.int32, (BF16_ROWS, 1), 0)
        for half in (0, 1):
            qt_ref[half, 0:LANES, :] = (jnp.where(row_lo, qt, zero) if half == 0
                                        else jnp.where(row_lo, zero, qt)).astype(BF16)
            if mode == "diff":
                qpos = q_start + lax.broadcasted_iota(jnp.int32, (1, tq), 1)
                b = -e1_ref[h] * qpos.astype(F32)
            else:
                b = e1_ref[0, 0][half:half + 1, :]
            b1, b2, b3 = (z.astype(F32) for z in _split3(b))
            own = (r16 >= 3 * half) & (r16 < 3 * half + 3)
            ext = jnp.where(own, 1.0, jnp.where(r16 == EXT_ONES_K, b1, jnp.where(
                r16 == EXT_ONES_K + 1, b2, jnp.where(r16 == EXT_ONES_K + 2, b3, 0.0))))
            qt_ref[half, LANES:LANES + BF16_ROWS, :] = ext.astype(BF16)
            qt_ref[half, LANES + BF16_ROWS:, :] = jnp.zeros((LANES - BF16_ROWS, tq), BF16)
        m_ref[...] = jnp.full(m_ref.shape, NEG_INF, F32)
        acc_ref[...] = jnp.zeros(acc_ref.shape, F32)

    def step(masked):
        kaug = jnp.concatenate([k_ref[0].astype(BF16), kx_ref[0]], axis=1)
        vt = jnp.concatenate([vt_ref[0], jnp.ones((BF16_ROWS, tk), BF16)], axis=0)
        if masked:
            kpos = k_start + lax.broadcasted_iota(jnp.int32, (tk, tq), 0)
            qpos = q_start + lax.broadcasted_iota(jnp.int32, (tk, tq), 1)
            if mode == "diff":
                mask = (kpos >> 6) <= (qpos >> 6)
                corr = (-2.0 * e1_ref[h]) * jnp.maximum(kpos - qpos, 0).astype(F32)
            else:
                mask = kpos <= qpos
        for half in (0, 1):
            s = jnp.dot(kaug, qt_ref[half], preferred_element_type=F32)
            if masked:
                s = jnp.where(mask, (s + corr) if mode == "diff" else s, NEG_INF)
            m_prev = m_ref[half]
            m_new = jnp.maximum(m_prev, jnp.max(s, axis=0, keepdims=True))
            alpha = jnp.exp(m_prev - m_new)
            pr = jnp.exp(s - m_new).astype(BF16)
            acc_ref[half] = alpha * acc_ref[half] + jnp.dot(vt, pr, preferred_element_type=F32)
            m_ref[half] = m_new

    interior = (k_start + tk) <= q_start

    @pl.when(interior)
    def _():
        step(False)

    @pl.when(jnp.logical_not(interior))
    def _():
        step(True)

    @pl.when(last_tab[p] == 1)
    def _fin():
        a0 = acc_ref[0]
        a1 = acc_ref[1]
        o0 = a0[0:LANES] * (1.0 / a0[LANES:LANES + 1])
        o1 = a1[0:LANES] * (1.0 / a1[LANES:LANES + 1])
        if mode == "diff":
            a = e2_ref[...]
            lam = (jnp.exp(jnp.sum(a[0:1] * a[1:2], axis=1, keepdims=True))
                   - jnp.exp(jnp.sum(a[2:3] * a[3:4], axis=1, keepdims=True)) + lam_init)
            o = (o0 - lam * o1).T
            o_ref[0] = (_rms(o, e3_ref[...]) * (1.0 - lam_init)).astype(o_ref.dtype)
        else:
            o_ref[0] = jnp.where(row_lo, o0, o1).T.astype(o_ref.dtype)


def _flash(mode, q, k, kx, vt, e1, e2, e3, *, tq, tk, q_off, lam_init=0.0):
    b, t_q, _ = q.shape
    n_col = q.shape[2] // LANES
    pairs = [(qi, kj) for qi in range(t_q // tq)
             for kj in range((q_off + (qi + 1) * tq - 1) // tk + 1)]
    qi_tab = np.asarray([pq[0] for pq in pairs], np.int32)
    kj_tab = np.asarray([pq[1] for pq in pairs], np.int32)
    last_tab = np.asarray([1 if (i + 1 == len(pairs) or pairs[i + 1][0] != pairs[i][0]) else 0
                           for i in range(len(pairs))], np.int32)
    const2 = lambda shape: pl.BlockSpec(shape, lambda bb, hh, pp, qt, kt, lt: (0, 0))
    if mode == "diff":
        kx_spec = pl.BlockSpec((1, tk, LANES), lambda bb, hh, pp, qt, kt, lt: (hh, kt[pp], 0))
        e_specs = [pl.BlockSpec(memory_space=pltpu.SMEM), const2(e2.shape), const2(e3.shape)]
    else:
        q_blk0 = q_off // tq
        kx_spec = pl.BlockSpec((1, tk, LANES), lambda bb, hh, pp, qt, kt, lt: (bb, kt[pp], hh))
        e_specs = [pl.BlockSpec((1, 1, 2, tq), lambda bb, hh, pp, qt, kt, lt: (bb, hh, 0, q_blk0 + qt[pp])),
                   const2(e2.shape), const2(e3.shape)]
    grid_spec = pltpu.PrefetchScalarGridSpec(
        num_scalar_prefetch=3,
        grid=(b, n_col, len(pairs)),
        in_specs=[pl.BlockSpec((1, tq, LANES), lambda bb, hh, pp, qt, kt, lt: (bb, qt[pp], hh)),
                  pl.BlockSpec((1, tk, LANES), lambda bb, hh, pp, qt, kt, lt: (bb, kt[pp], hh)),
                  kx_spec,
                  pl.BlockSpec((1, LANES, tk), lambda bb, hh, pp, qt, kt, lt: (bb, hh, kt[pp]))] + e_specs,
        out_specs=pl.BlockSpec((1, tq, LANES), lambda bb, hh, pp, qt, kt, lt: (bb, qt[pp], hh)),
        scratch_shapes=[pltpu.VMEM((2, 2 * LANES, tq), BF16),
                        pltpu.VMEM((2, 1, tq), F32),
                        pltpu.VMEM((2, LANES + BF16_ROWS, tq), F32)],
    )
    kern = functools.partial(_flash_kernel, mode=mode, tq=tq, tk=tk, q_off=q_off, lam_init=lam_init)
    return pl.pallas_call(
        kern,
        grid_spec=grid_spec,
        out_shape=jax.ShapeDtypeStruct(q.shape, BF16),
        compiler_params=_params(3),
        name="flash_" + mode,
    )(qi_tab, kj_tab, last_tab, q, k, kx, vt, e1, e2, e3)


def _alibi_key_cols(slopes, t_k):
    pos = jnp.arange(t_k, dtype=jnp.int32)
    a1 = slopes[:, None] * ((pos >> 7) << 7).astype(F32)[None, :]
    a2 = slopes[:, None] * (pos & 127).astype(F32)[None, :]
    zero = jnp.zeros_like(a1)
    one = jnp.ones_like(a1)
    cols = jnp.stack([a1, a2, zero, a1, a2, zero, one, one, one], axis=-1)
    return jnp.pad(cols, ((0, 0), (0, 0), (0, LANES - EXT_USED))).astype(BF16)


def _band_kernel(q_ref, *rest, nblk, rows, thr0):
    k_refs = rest[:nblk]
    v_refs = rest[nblk:2 * nblk]
    bias_ref = rest[2 * nblk]
    o_ref = rest[2 * nblk + 1]
    c = pl.program_id(1)
    lk = nblk * rows
    valid = lax.broadcasted_iota(jnp.int32, (1, lk), 1) >= (thr0 - c * rows)
    lo = lax.broadcasted_iota(jnp.int32, (1, LANES), 1) < HEAD_DIM
    for pair in range(B_HEADS // 2):
        sl = slice(LANES * pair, LANES * pair + LANES)
        q = q_ref[0, :, sl]
        zero = jnp.zeros_like(q)
        k = jnp.concatenate([r[0, :, sl] for r in k_refs], axis=0).astype(BF16)
        v = jnp.concatenate([r[0, :, sl] for r in v_refs], axis=0).astype(BF16)
        outs = []
        for half in (0, 1):
            qh = jnp.where(lo, q, zero) if half == 0 else jnp.where(lo, zero, q)
            s = lax.dot_general(qh, k, (((1,), (1,)), ((), ())), preferred_element_type=F32)
            s = jnp.where(valid, s + bias_ref[2 * pair + half], NEG_INF)
            m = jnp.max(s, axis=1, keepdims=True)
            pr = jnp.exp(s - m)
            den = jnp.sum(pr, axis=1, keepdims=True)
            outs.append(jnp.dot(pr.astype(BF16), v, preferred_element_type=F32) * (1.0 / den))
        o_ref[0, :, sl] = jnp.where(lo, outs[0], outs[1]).astype(o_ref.dtype)


def _band_keys(rows):
    return -(-(B_WIN + rows) // LANES) * LANES


def _band_bias(table, rows):
    lk = _band_keys(rows)
    w = rows + lk - 1
    d = np.concatenate([np.arange(lk), np.arange(-(rows - 1), 0)])
    idx = np.clip(B_WIN - d, -B_REL_CLIP, B_REL_CLIP) + B_REL_CLIP
    e = table.astype(F32)[:, idx]
    toep = jnp.tile(e, (1, rows))[:, :rows * (w - 1)].reshape(-1, rows, w - 1)[:, :, :lk]
    i = np.arange(rows)[:, None]
    j = np.arange(lk)[None, :]
    lo_edge = (i // CHUNK) * CHUNK
    inband = (j >= lo_edge) & (j < lo_edge + B_WIN + CHUNK)
    return jnp.where(jnp.asarray(inband)[None], toep, NEG_INF)


def _band(q, k, v, table, *, rows, off):
    b, t, w = q.shape
    nblk = _band_keys(rows) // rows
    shift = (B_WIN - off) // rows
    last = k.shape[1] // rows - 1
    bias = _band_bias(table, rows)
    kv_specs = [pl.BlockSpec((1, rows, w), lambda bb, cc, i=i: (bb, jnp.clip(cc + i - shift, 0, last), 0))
                for i in range(nblk)]
    kern = functools.partial(_band_kernel, nblk=nblk, rows=rows, thr0=B_WIN - off)
    return pl.pallas_call(
        kern,
        grid=(b, t // rows),
        in_specs=[pl.BlockSpec((1, rows, w), lambda bb, cc: (bb, cc, 0))] + kv_specs + kv_specs
                 + [pl.BlockSpec(bias.shape, lambda bb, cc: (0, 0, 0))],
        out_specs=pl.BlockSpec((1, rows, w), lambda bb, cc: (bb, cc, 0)),
        out_shape=jax.ShapeDtypeStruct(q.shape, BF16),
        compiler_params=_params(2),
        name="band_attn",
    )(q, *([k] * nblk), *([v] * nblk), bias)


def _gelu_tanh(x):
    return 0.5 * x * (1.0 + jnp.tanh(math.sqrt(2.0 / math.pi) * (x + 0.044715 * (x * x * x))))


def _rglru_kernel(dx_ref, dg_ref, buf_ref, h0_ref, cw_ref, cb_ref, wa_ref, ba_ref, wx_ref, bx_ref, lam_ref,
                  od_ref, nbuf_ref, hl_ref, tail_ref, hst_ref, a_s, b_s, h_s, *, tt):
    j = pl.program_id(1)
    nj = pl.num_programs(1)

    @pl.when(j == 0)
    def _():
        tail_ref[...] = jnp.zeros(tail_ref.shape, F32)
        tail_ref[5:8, :] = buf_ref[0]
        hst_ref[...] = h0_ref[0]

    x = dx_ref[0]
    xs = jnp.concatenate([tail_ref[...], x], axis=0)
    cw = cw_ref[...]
    u = cb_ref[...] + cw[3:4] * x
    for tap in range(D_CONV - 1):
        u = u + cw[tap:tap + 1] * xs[5 + tap:5 + tap + tt]
    tail_ref[...] = x[tt - 8:tt]

    ub = u.astype(BF16)
    r = _sigmoid(jnp.dot(ub, wa_ref[...], preferred_element_type=F32) + ba_ref[...])
    gate_i = _sigmoid(jnp.dot(ub, wx_ref[...], preferred_element_type=F32) + bx_ref[...])
    log_a = -RG_C * r * _softplus(-lam_ref[...])
    a_s[...] = jnp.exp(log_a)
    b_s[...] = jnp.sqrt(-_expm1(2.0 * log_a)) * (gate_i * u)

    def body(i, h):
        base = pl.multiple_of(i * 8, 8)
        for rr in range(8):
            h = a_s[pl.ds(base + rr, 1), :] * h + b_s[pl.ds(base + rr, 1), :]
            h_s[pl.ds(base + rr, 1), :] = h
        return h

    h_fin = lax.fori_loop(0, tt // 8, body, hst_ref[...])
    hst_ref[...] = h_fin
    od_ref[0] = (h_s[...] * _gelu_tanh(dg_ref[0])).astype(od_ref.dtype)

    @pl.when(j == nj - 1)
    def _():
        nbuf_ref[0] = tail_ref[8 - (D_CONV - 1):8, :]
        hl_ref[0] = h_fin


def _rglru(dx, dg, buf, h0, cw, cb, wa, ba, wx, bx, lam, *, tt):
    b, t, w = dx.shape
    const2 = lambda shape: pl.BlockSpec(shape, lambda bb, jj: (0, 0))
    return pl.pallas_call(
        functools.partial(_rglru_kernel, tt=tt),
        grid=(b, t // tt),
        in_specs=[pl.BlockSpec((1, tt, w), lambda bb, jj: (bb, jj, 0)),
                  pl.BlockSpec((1, tt, w), lambda bb, jj: (bb, jj, 0)),
                  pl.BlockSpec((1, D_CONV - 1, w), lambda bb, jj: (bb, 0, 0)),
                  pl.BlockSpec((1, 1, w), lambda bb, jj: (bb, 0, 0)),
                  const2((D_CONV, w)), const2((1, w)), const2((w, w)), const2((1, w)),
                  const2((w, w)), const2((1, w)), const2((1, w))],
        out_specs=[pl.BlockSpec((1, tt, w), lambda bb, jj: (bb, jj, 0)),
                   pl.BlockSpec((1, D_CONV - 1, w), lambda bb, jj: (bb, 0, 0)),
                   pl.BlockSpec((1, 1, w), lambda bb, jj: (bb, 0, 0))],
        out_shape=[jax.ShapeDtypeStruct((b, t, w), BF16),
                   jax.ShapeDtypeStruct((b, D_CONV - 1, w), F32),
                   jax.ShapeDtypeStruct((b, 1, w), F32)],
        scratch_shapes=[pltpu.VMEM((8, w), F32), pltpu.VMEM((1, w), F32),
                        pltpu.VMEM((tt, w), F32), pltpu.VMEM((tt, w), F32), pltpu.VMEM((tt, w), F32)],
        compiler_params=_params(2),
        name="rglru",
    )(dx, dg, buf, h0, cw, cb, wa, ba, wx, bx, lam)


def _tail_kernel(x_ref, o1_ref, o2_ref, wo_ref, g_ref, w1_ref, w3_ref, w2_ref, fg_ref, out_ref,
                 y_ref, h_ref, acc_ref, *, final_norm):
    j = pl.program_id(1)
    half = wo_ref.shape[0] // 2

    @pl.when(j == 0)
    def _():
        y = (x_ref[...]
             + jnp.dot(o1_ref[...], wo_ref[0:half, :], preferred_element_type=F32)
             + jnp.dot(o2_ref[...], wo_ref[half:, :], preferred_element_type=F32))
        y_ref[...] = y
        h_ref[...] = _rms(y, g_ref[...]).astype(BF16)
        acc_ref[...] = jnp.zeros(acc_ref.shape, F32)

    h = h_ref[...]
    a = jnp.dot(h, w1_ref[...], preferred_element_type=F32)
    g = jnp.dot(h, w3_ref[...], preferred_element_type=F32)
    act = (a * _sigmoid(a) * g).astype(BF16)
    acc_ref[...] += jnp.dot(act, w2_ref[...], preferred_element_type=F32)

    @pl.when(j == pl.num_programs(1) - 1)
    def _():
        y = y_ref[...] + acc_ref[...]
        if final_norm:
            y = _rms(y, fg_ref[...])
        out_ref[...] = y


def _tail(x, o1, o2, wo, g, w1, w3, w2, fg, *, final_norm, tm, hc):
    n = x.shape[0]
    hw = o1.shape[1]
    return pl.pallas_call(
        functools.partial(_tail_kernel, final_norm=final_norm),
        grid=(n // tm, FFN_HIDDEN // hc),
        in_specs=[pl.BlockSpec((tm, D_MODEL), lambda i, j: (i, 0)),
                  pl.BlockSpec((tm, hw), lambda i, j: (i, 0)),
                  pl.BlockSpec((tm, hw), lambda i, j: (i, 0)),
                  pl.BlockSpec(wo.shape, lambda i, j: (0, 0)),
                  pl.BlockSpec((1, D_MODEL), lambda i, j: (0, 0)),
                  pl.BlockSpec((D_MODEL, hc), lambda i, j: (0, j)),
                  pl.BlockSpec((D_MODEL, hc), lambda i, j: (0, j)),
                  pl.BlockSpec((hc, D_MODEL), lambda i, j: (j, 0)),
                  pl.BlockSpec((1, D_MODEL), lambda i, j: (0, 0))],
        out_specs=pl.BlockSpec((tm, D_MODEL), lambda i, j: (i, 0)),
        out_shape=jax.ShapeDtypeStruct((n, D_MODEL), F32),
        scratch_shapes=[pltpu.VMEM((tm, D_MODEL), F32), pltpu.VMEM((tm, D_MODEL), BF16),
                        pltpu.VMEM((tm, D_MODEL), F32)],
        compiler_params=_params(2, 56),
        name="tail_ffn",
    )(x, o1, o2, wo, g, w1, w3, w2, fg)


def _pad_time(x, front, back):
    return jnp.pad(x, ((0, 0), (front, back), (0, 0)))


def _block_diag(w):
    n, d, _ = w.shape
    out = jnp.zeros((n * d, n * d), w.dtype)
    for i in range(n):
        out = out.at[i * d:(i + 1) * d, i * d:(i + 1) * d].set(w[i])
    return out


def _transposed_bf16(v):
    return jnp.swapaxes(v, 1, 2).astype(BF16)


def _pad_queries(q, tq):
    return _pad_time(q, 0, (-q.shape[1]) % tq)


def _layer_ab(x, past, wts, *, tm, tq, tk, rows):
    b, t, _ = x.shape
    n = b * t
    w_in, slopes, a_lambda, a_subln_g, b_rel_bias, g_mix = wts
    aq, ak, av, bq, bk, bv = _norm_proj(
        x.reshape(n, D_MODEL), g_mix, w_in, jnp.zeros((1, LANES), F32),
        [512] * 6, [BF16, F32, F32, BF16, F32, F32], [SCALE, 1.0, 1.0, SCALE, 1.0, 1.0], False, tm)
    r3 = lambda z: z.reshape(b, t, 512)
    aq, ak, av, bq, bk, bv = map(r3, (aq, ak, av, bq, bk, bv))
    if past is None:
        ka, va, q_off = ak, av, 0
        kb, vb, off = bk, bv, 0
        keep = min(B_WIN, t)
        nbk, nbv = bk[:, t - keep:], bv[:, t - keep:]
    else:
        cak, cav, cbk, cbv = past
        p_len = cak.shape[1]
        back = (-(p_len + t)) % tk
        ka = _pad_time(jnp.concatenate([cak.reshape(b, p_len, 512), ak], axis=1), 0, back)
        va = _pad_time(jnp.concatenate([cav.reshape(b, p_len, 512), av], axis=1), 0, back)
        q_off = p_len
        lb = cbk.shape[1]
        kb = jnp.concatenate([cbk.reshape(b, lb, 512), bk], axis=1)
        vb = jnp.concatenate([cbv.reshape(b, lb, 512), bv], axis=1)
        off = lb
        nbk, nbv = kb[:, t:], vb[:, t:]
    o_a = _flash("diff", _pad_queries(aq, tq), ka, _alibi_key_cols(slopes, ka.shape[1]), _transposed_bf16(va),
                 slopes, a_lambda, a_subln_g.reshape(1, LANES),
                 tq=tq, tk=tk, q_off=q_off, lam_init=_lambda_init(0))[:, :t]
    o_b = _band(bq, kb, vb, b_rel_bias, rows=rows, off=off)
    state = (ak.reshape(b, t, A_HEADS, 2 * HEAD_DIM), av.reshape(b, t, A_HEADS, 2 * HEAD_DIM),
             nbk.reshape(b, -1, B_HEADS, HEAD_DIM), nbv.reshape(b, -1, B_HEADS, HEAD_DIM))
    return o_a.reshape(n, 512), o_b.reshape(n, 512), state


def _layer_cd(x, past, wts, *, tm, tq, tk, tt, tc):
    b, t, _ = x.shape
    n = b * t
    w_in, fb, cw, cb, wa, ba, wx, bx, lam, g_mix = wts
    cq, ck, cv, dx, dg, logf = _norm_proj(
        x.reshape(n, D_MODEL), g_mix, w_in, fb,
        [512] * 5 + [LANES], [BF16, F32, F32, F32, F32, F32], [SCALE, 1.0, 1.0, 1.0, 1.0, 1.0], True, tm)
    r3 = lambda z: z.reshape(b, t, z.shape[-1])
    cq, ck, cv, dx, dg, logf = map(r3, (cq, ck, cv, dx, dg, logf))
    if past is None:
        kc, vc, lf_all, q_off = ck, cv, logf, 0
        buf = jnp.zeros((b, D_CONV - 1, D_WIDTH), F32)
        h0 = jnp.zeros((b, 1, D_WIDTH), F32)
    else:
        cck, ccv, cclogf, buf, h0 = past
        p_len = cck.shape[1]
        back = (-(p_len + t)) % tk
        kc = _pad_time(jnp.concatenate([cck.reshape(b, p_len, 512), ck], axis=1), 0, back)
        vc = _pad_time(jnp.concatenate([ccv.reshape(b, p_len, 512), cv], axis=1), 0, back)
        q_off = p_len
        lf_all = _pad_time(jnp.concatenate(
            [jnp.pad(cclogf.astype(F32), ((0, 0), (0, 0), (0, LANES - C_HEADS))), logf], axis=1), 0, back)
        h0 = h0.reshape(b, 1, D_WIDTH)
    kx, f_rows = _fox_prep(lf_all, tc)
    f_rows = f_rows.reshape(b, C_HEADS // 2, 2, -1)
    dummy = jnp.zeros((1, LANES), F32)
    o_c = _flash("fox", _pad_queries(cq, tq), kc, kx, _transposed_bf16(vc), f_rows, dummy, dummy,
                 tq=tq, tk=tk, q_off=q_off)[:, :t]
    o_d, nbuf, hl = _rglru(dx, dg, buf, h0, cw, cb, wa, ba, wx, bx, lam, tt=tt)
    state = (ck.reshape(b, t, C_HEADS, HEAD_DIM), cv.reshape(b, t, C_HEADS, HEAD_DIM),
             logf[:, :, :C_HEADS], nbuf, hl.reshape(b, D_WIDTH))
    return o_c.reshape(n, 512), o_d.reshape(n, 512), state


def kernel(x_prompt, x_sample, cache_a_k, cache_a_v, cache_b_k, cache_b_v, cache_c_k, cache_c_v, cache_c_logf, state_d_conv, state_d_h, norm_mix_g, norm_ffn_g, ab_w_in, ab_w_out, a_lambda, a_subln_g, b_rel_bias, cd_w_in, cd_w_out, c_f_bias, d_conv_w, d_conv_b, d_w_a, d_b_a, d_w_x, d_b_x, d_lambda, ffn_w1, ffn_w3, ffn_w2, final_g):
    slopes = jnp.asarray([2.0 ** (-8.0 * (h + 1) / A_HEADS) for h in range(A_HEADS)], F32)
    row = lambda z: z.reshape(1, -1).astype(F32)

    ab_wts = (ab_w_in.astype(BF16), slopes, a_lambda.astype(F32), a_subln_g.astype(F32), b_rel_bias,
              row(norm_mix_g[0]))
    w_cd = jnp.concatenate([cd_w_in[:, :1536], cd_w_in[:, 1544:], cd_w_in[:, 1536:1544],
                            jnp.zeros((D_MODEL, LANES - C_HEADS), cd_w_in.dtype)], axis=1).astype(BF16)
    fb = jnp.pad(c_f_bias.astype(F32), (0, LANES - C_HEADS)).reshape(1, LANES)
    cd_wts = (w_cd, fb, d_conv_w.astype(F32), row(d_conv_b), _block_diag(d_w_a).astype(BF16), row(d_b_a),
              _block_diag(d_w_x).astype(BF16), row(d_b_x), row(d_lambda), row(norm_mix_g[1]))
    wo = (ab_w_out.astype(BF16), cd_w_out.astype(BF16))
    w1, w3, w2 = ffn_w1.astype(BF16), ffn_w3.astype(BF16), ffn_w2.astype(BF16)
    fg = row(final_g)

    def trunk(x, past_ab, past_cd, cfg):
        b, t, _ = x.shape
        o_a, o_b, st_ab = _layer_ab(x, past_ab, ab_wts, tm=cfg["tm"], tq=cfg["tq"], tk=cfg["tk"],
                                    rows=cfg["rows"])
        y = _tail(x.reshape(b * t, D_MODEL), o_a, o_b, wo[0], row(norm_ffn_g[0]), w1[0], w3[0], w2[0], fg,
                  final_norm=False, tm=cfg["tm"], hc=cfg["hc"])
        o_c, o_d, st_cd = _layer_cd(y.reshape(b, t, D_MODEL), past_cd, cd_wts, tm=cfg["tm"], tq=cfg["tq"],
                                    tk=cfg["tk"], tt=cfg["tt"], tc=cfg["tc"])
        y = _tail(y, o_c, o_d, wo[1], row(norm_ffn_g[1]), w1[1], w3[1], w2[1], fg,
                  final_norm=True, tm=cfg["tm"], hc=cfg["hc"])
        return (y.reshape(b, t, D_MODEL),) + st_ab + st_cd

    prompt_cfg = dict(tm=512, tq=512, tk=512, rows=256, hc=1408, tt=512, tc=512)
    sample_cfg = dict(tm=512, tq=128, tk=512, rows=64, hc=1408, tt=64, tc=512)
    outp = trunk(x_prompt, None, None, prompt_cfg)
    outs = trunk(x_sample, (cache_a_k, cache_a_v, cache_b_k, cache_b_v),
                 (cache_c_k, cache_c_v, cache_c_logf, state_d_conv, state_d_h), sample_cfg)
    return (outp[0], outs[0]) + outp[1:] + outs[1:]
```

```python
import functools
import math

import numpy as np
import jax
import jax.numpy as jnp
from jax import lax
from jax.experimental import pallas as pl
from jax.experimental.pallas import tpu as pltpu

F32 = jnp.float32
BF16 = jnp.bfloat16

D_MODEL = 1024
CHUNK = 64
HEAD_DIM = 64
EPS = 1e-6
NEG_INF = -1e30
SCALE = HEAD_DIM ** -0.5
A_HEADS = 4
B_HEADS = 8
B_WIN = 512
B_REL_CLIP = 128
C_HEADS = 8
D_WIDTH = 512
D_CONV = 4
RG_C = 8.0
FFN_HIDDEN = 2816
LANES = 128
BF16_ROWS = 16
MIB = 1024 * 1024

EXT_ONES_K = 6
EXT_USED = 9


def _lambda_init(layer):
    return 0.8 - 0.6 * math.exp(-0.3 * layer)


def _params(n_axes, vmem_mib=48):
    return pltpu.CompilerParams(dimension_semantics=("arbitrary",) * n_axes,
                                vmem_limit_bytes=vmem_mib * MIB)


def _rms(x, g):
    return x * lax.rsqrt(jnp.mean(x * x, axis=-1, keepdims=True) + EPS) * g


def _sigmoid(x):
    return 1.0 / (1.0 + jnp.exp(-x))


def _expm1(x):
    u = jnp.exp(x)
    return jnp.where(u == 1.0, x, (u - 1.0) * x / jnp.log(jnp.where(u == 1.0, 2.0, u)))


def _softplus(x):
    return jnp.maximum(x, 0.0) + jnp.log1p(jnp.exp(-jnp.abs(x)))


def _split3(x):
    x1 = x.astype(BF16)
    r1 = x - x1.astype(F32)
    x2 = r1.astype(BF16)
    x3 = (r1 - x2.astype(F32)).astype(BF16)
    return x1, x2, x3


def _norm_proj_kernel(x_ref, g_ref, w_ref, fb_ref, *out_refs, widths, scales, logsig_last):
    h = _rms(x_ref[...], g_ref[...]).astype(BF16)
    off = 0
    for idx, (o_ref, wd) in enumerate(zip(out_refs, widths)):
        y = jnp.dot(h, w_ref[:, off:off + wd], preferred_element_type=F32)
        if logsig_last and idx == len(widths) - 1:
            y = -_softplus(-(y + fb_ref[...]))
        if scales[idx] != 1.0:
            y = y * scales[idx]
        o_ref[...] = y.astype(o_ref.dtype)
        off += wd


def _norm_proj(x, g, w, fb, widths, dtypes, scales, logsig_last, tm):
    n = x.shape[0]
    kern = functools.partial(_norm_proj_kernel, widths=tuple(widths), scales=tuple(scales),
                             logsig_last=logsig_last)
    return pl.pallas_call(
        kern,
        grid=(n // tm,),
        in_specs=[pl.BlockSpec((tm, D_MODEL), lambda i: (i, 0)),
                  pl.BlockSpec((1, D_MODEL), lambda i: (0, 0)),
                  pl.BlockSpec(w.shape, lambda i: (0, 0)),
                  pl.BlockSpec((1, LANES), lambda i: (0, 0))],
        out_specs=[pl.BlockSpec((tm, wd), lambda i: (i, 0)) for wd in widths],
        out_shape=[jax.ShapeDtypeStruct((n, wd), dt) for wd, dt in zip(widths, dtypes)],
        compiler_params=_params(1),
        name="norm_proj",
    )(x, g, w, fb)


def _fox_prep_kernel(x_ref, sel_ref, ones_ref, kx_ref, ft_ref, carry_ref, *, tt):
    @pl.when(pl.program_id(1) == 0)
    def _():
        carry_ref[...] = jnp.zeros(carry_ref.shape, F32)

    lane = lax.broadcasted_iota(jnp.int32, (1, LANES), 1)
    x = jnp.where(lane < C_HEADS, x_ref[0], 0.0)
    row = lax.broadcasted_iota(jnp.int32, (tt, tt), 0)
    col = lax.broadcasted_iota(jnp.int32, (tt, tt), 1)
    tri = jnp.where(col <= row, 1.0, 0.0).astype(BF16)
    x1, x2, x3 = _split3(x)
    f = (jnp.dot(tri, x1, preferred_element_type=F32) + jnp.dot(tri, x2, preferred_element_type=F32)
         + jnp.dot(tri, x3, preferred_element_type=F32)) + carry_ref[0:1, :]
    carry_ref[0:1, :] = f[tt - 1:tt, :]
    f1, f2, f3 = _split3(f)
    pieces = jnp.concatenate([f1, f2, f3], axis=1)
    kx_ref[0] = (ones_ref[...] - jnp.dot(pieces, sel_ref[...], preferred_element_type=F32)).astype(BF16)
    r = lax.broadcasted_iota(jnp.int32, (BF16_ROWS, LANES), 0)
    c = lax.broadcasted_iota(jnp.int32, (BF16_ROWS, LANES), 1)
    eye = jnp.where(r == c, 1.0, 0.0).astype(BF16)
    dn = (((1,), (1,)), ((), ()))
    ft = (lax.dot_general(eye, f1, dn, preferred_element_type=F32)
          + lax.dot_general(eye, f2, dn, preferred_element_type=F32)
          + lax.dot_general(eye, f3, dn, preferred_element_type=F32))
    ft_ref[0] = ft[0:C_HEADS, :]


def _fox_sel():
    sel = np.zeros((3 * LANES, C_HEADS // 2 * LANES), np.float32)
    ones = np.zeros((1, C_HEADS // 2 * LANES), np.float32)
    for pair in range(C_HEADS // 2):
        for half in (0, 1):
            for piece in range(3):
                sel[piece * LANES + 2 * pair + half, pair * LANES + 3 * half + piece] = 1.0
        ones[0, pair * LANES + EXT_ONES_K:pair * LANES + EXT_USED] = 1.0
    return jnp.asarray(sel, BF16), jnp.asarray(ones, F32)


def _fox_prep(logf, tt):
    b, t, _ = logf.shape
    sel, ones = _fox_sel()
    return pl.pallas_call(
        functools.partial(_fox_prep_kernel, tt=tt),
        grid=(b, t // tt),
        in_specs=[pl.BlockSpec((1, tt, LANES), lambda i, j: (i, j, 0)),
                  pl.BlockSpec(sel.shape, lambda i, j: (0, 0)),
                  pl.BlockSpec(ones.shape, lambda i, j: (0, 0))],
        out_specs=[pl.BlockSpec((1, tt, sel.shape[1]), lambda i, j: (i, j, 0)),
                   pl.BlockSpec((1, C_HEADS, tt), lambda i, j: (i, 0, j))],
        out_shape=[jax.ShapeDtypeStruct((b, t, sel.shape[1]), BF16),
                   jax.ShapeDtypeStruct((b, C_HEADS, t), F32)],
        scratch_shapes=[pltpu.VMEM((8, LANES), F32)],
        compiler_params=_params(2),
        name="fox_prep",
    )(logf, sel, ones)


def _flash_kernel(q_ref, k_ref, kx_ref, vt_ref, mb_ref, e1_ref, e2_ref, e3_ref, o_ref,
                  qt_ref, m_ref, acc_ref, s0_ref, s1_ref, p0_ref, p1_ref, al0_ref, al1_ref,
                  *, mode, tq, tk, q_off, lam_init):
    h = pl.program_id(1)
    q_start = q_off + pl.program_id(2) * tq
    row_lo = lax.broadcasted_iota(jnp.int32, (LANES, 1), 0) < HEAD_DIM

    qt = q_ref[0].astype(F32).T
    zero = jnp.zeros_like(qt)
    r16 = lax.broadcasted_iota(jnp.int32, (BF16_ROWS, 1), 0)
    for half in (0, 1):
        qt_ref[half, 0:LANES, :] = (jnp.where(row_lo, qt, zero) if half == 0
                                    else jnp.where(row_lo, zero, qt)).astype(BF16)
        if mode == "diff":
            qpos = q_start + lax.broadcasted_iota(jnp.int32, (1, tq), 1)
            b = -e1_ref[h] * qpos.astype(F32)
        else:
            b = e1_ref[0, 0][half:half + 1, :]
        b1, b2, b3 = (z.astype(F32) for z in _split3(b))
        own = (r16 >= 3 * half) & (r16 < 3 * half + 3)
        ext = jnp.where(own, 1.0, jnp.where(r16 == EXT_ONES_K, b1, jnp.where(
            r16 == EXT_ONES_K + 1, b2, jnp.where(r16 == EXT_ONES_K + 2, b3, 0.0))))
        qt_ref[half, LANES:LANES + BF16_ROWS, :] = ext.astype(BF16)
        qt_ref[half, LANES + BF16_ROWS:, :] = jnp.zeros((LANES - BF16_ROWS, tq), BF16)
    m_ref[...] = jnp.full(m_ref.shape, NEG_INF, F32)
    acc_ref[...] = jnp.zeros(acc_ref.shape, F32)
    p1_ref[...] = jnp.zeros(p1_ref.shape, BF16)
    al1_ref[...] = jnp.ones(al1_ref.shape, F32)

    def scores(t, s_ref):
        ks = pl.multiple_of(t * tk, tk)
        kaug = jnp.concatenate([k_ref[0, pl.ds(ks, tk), :].astype(BF16), kx_ref[0, pl.ds(ks, tk), :]], axis=1)
        for half in (0, 1):
            s_ref[half] = jnp.dot(kaug, qt_ref[half], preferred_element_type=F32)

    def softmax(t, s_ref, p_ref, al_ref, masked):
        for half in (0, 1):
            s = s_ref[half]
            if masked:
                s = s + mb_ref[0]
            m_prev = m_ref[half]
            m_new = jnp.maximum(m_prev, jnp.max(s, axis=0, keepdims=True))
            al_ref[half] = jnp.exp(m_prev - m_new)
            p_ref[half] = jnp.exp(s - m_new).astype(BF16)
            m_ref[half] = m_new

    def values(t, p_ref, al_ref):
        vt = jnp.concatenate([vt_ref[0, 0, jnp.maximum(t, 0)], jnp.ones((BF16_ROWS, tk), BF16)], axis=0)
        for half in (0, 1):
            acc_ref[half] = al_ref[half] * acc_ref[half] + jnp.dot(vt, p_ref[half], preferred_element_type=F32)

    n_int = lax.div(q_start, tk)
    n_pairs = lax.div(n_int, 2)

    def pair_body(i, carry):
        t = 2 * i
        values(t - 1, p1_ref, al1_ref)
        scores(t + 1, s1_ref)
        softmax(t, s0_ref, p0_ref, al0_ref, False)
        values(t, p0_ref, al0_ref)
        scores(t + 2, s0_ref)
        softmax(t + 1, s1_ref, p1_ref, al1_ref, False)
        return carry

    scores(0, s0_ref)
    lax.fori_loop(0, n_pairs, pair_body, 0)
    t0 = 2 * n_pairs

    @pl.when(n_int > t0)
    def _():
        values(t0 - 1, p1_ref, al1_ref)
        scores(t0 + 1, s1_ref)
        softmax(t0, s0_ref, p0_ref, al0_ref, False)
        values(t0, p0_ref, al0_ref)
        softmax(t0 + 1, s1_ref, p1_ref, al1_ref, True)
        values(t0 + 1, p1_ref, al1_ref)

    @pl.when(n_int == t0)
    def _():
        values(t0 - 1, p1_ref, al1_ref)
        softmax(t0, s0_ref, p0_ref, al0_ref, True)
        values(t0, p0_ref, al0_ref)

    a0 = acc_ref[0]
    a1 = acc_ref[1]
    o0 = a0[0:LANES] * (1.0 / a0[LANES:LANES + 1])
    o1 = a1[0:LANES] * (1.0 / a1[LANES:LANES + 1])
    if mode == "diff":
        a = e2_ref[...]
        lam = (jnp.exp(jnp.sum(a[0:1] * a[1:2], axis=1, keepdims=True))
               - jnp.exp(jnp.sum(a[2:3] * a[3:4], axis=1, keepdims=True)) + lam_init)
        o = (o0 - lam * o1).T
        o_ref[0] = (_rms(o, e3_ref[...]) * (1.0 - lam_init)).astype(o_ref.dtype)
    else:
        o_ref[0] = jnp.where(row_lo, o0, o1).T.astype(o_ref.dtype)


def _flash(mode, q, k, kx, v, e1, e2, e3, *, tq, tk, q_off, lam_init=0.0):
    b, t_q, _ = q.shape
    t_k = k.shape[1]
    n_col = q.shape[2] // LANES
    nt = t_k // tk
    assert tk % tq == 0 and q_off % tq == 0 and t_k >= q_off + t_q and (tq == tk or t_q == tq)
    kpos = jnp.arange(tk, dtype=jnp.int32)[:, None]
    qpos = q_off % tk + jnp.arange(tq, dtype=jnp.int32)[None, :]
    if mode == "diff":
        corr = -2.0 * e1[:, None, None] * jnp.maximum(kpos - qpos, 0).astype(F32)[None]
        mask_bias = jnp.where(((kpos >> 6) <= (qpos >> 6))[None], corr, NEG_INF)
        mb_spec = pl.BlockSpec((1, tk, tq), lambda bb, hh, qq: (hh, 0, 0))
    else:
        mask_bias = jnp.where(kpos <= qpos, 0.0, NEG_INF).astype(F32)[None]
        mb_spec = pl.BlockSpec((1, tk, tq), lambda bb, hh, qq: (0, 0, 0))
    vt = jnp.transpose(v.reshape(b, nt, tk, n_col, LANES), (0, 3, 1, 4, 2)).astype(BF16)
    const2 = lambda shape: pl.BlockSpec(shape, lambda bb, hh, qq: (0, 0))
    if mode == "diff":
        kx_spec = pl.BlockSpec((1, t_k, LANES), lambda bb, hh, qq: (hh, 0, 0))
        e1_spec = pl.BlockSpec(memory_space=pltpu.SMEM)
    else:
        q_blk0 = q_off // tq
        kx_spec = pl.BlockSpec((1, t_k, LANES), lambda bb, hh, qq: (bb, 0, hh))
        e1_spec = pl.BlockSpec((1, 1, 2, tq), lambda bb, hh, qq: (bb, hh, 0, q_blk0 + qq))
    kern = functools.partial(_flash_kernel, mode=mode, tq=tq, tk=tk, q_off=q_off, lam_init=lam_init)
    return pl.pallas_call(
        kern,
        grid=(b, n_col, t_q // tq),
        in_specs=[pl.BlockSpec((1, tq, LANES), lambda bb, hh, qq: (bb, qq, hh)),
                  pl.BlockSpec((1, t_k, LANES), lambda bb, hh, qq: (bb, 0, hh)),
                  kx_spec,
                  pl.BlockSpec((1, 1, nt, LANES, tk), lambda bb, hh, qq: (bb, hh, 0, 0, 0)),
                  mb_spec, e1_spec, const2(e2.shape), const2(e3.shape)],
        out_specs=pl.BlockSpec((1, tq, LANES), lambda bb, hh, qq: (bb, qq, hh)),
        out_shape=jax.ShapeDtypeStruct(q.shape, BF16),
        scratch_shapes=[pltpu.VMEM((2, 2 * LANES, tq), BF16),
                        pltpu.VMEM((2, 1, tq), F32),
                        pltpu.VMEM((2, LANES + BF16_ROWS, tq), F32),
                        pltpu.VMEM((2, tk, tq), F32), pltpu.VMEM((2, tk, tq), F32),
                        pltpu.VMEM((2, tk, tq), BF16), pltpu.VMEM((2, tk, tq), BF16),
                        pltpu.VMEM((2, 1, tq), F32), pltpu.VMEM((2, 1, tq), F32)],
        compiler_params=_params(3),
        name="flash_" + mode,
    )(q, k, kx, vt, mask_bias, e1, e2, e3)


def _alibi_key_cols(slopes, t_k):
    pos = jnp.arange(t_k, dtype=jnp.int32)
    a1 = slopes[:, None] * ((pos >> 7) << 7).astype(F32)[None, :]
    a2 = slopes[:, None] * (pos & 127).astype(F32)[None, :]
    zero = jnp.zeros_like(a1)
    one = jnp.ones_like(a1)
    cols = jnp.stack([a1, a2, zero, a1, a2, zero, one, one, one], axis=-1)
    return jnp.pad(cols, ((0, 0), (0, 0), (0, LANES - EXT_USED))).astype(BF16)


def _band_kernel(q_ref, *rest, nblk, rows, thr0):
    k_refs = rest[:nblk]
    v_refs = rest[nblk:2 * nblk]
    bias_ref = rest[2 * nblk]
    o_ref = rest[2 * nblk + 1]
    c = pl.program_id(1)
    lk = nblk * rows
    valid = lax.broadcasted_iota(jnp.int32, (1, lk), 1) >= (thr0 - c * rows)
    lo = lax.broadcasted_iota(jnp.int32, (1, LANES), 1) < HEAD_DIM
    for pair in range(B_HEADS // 2):
        sl = slice(LANES * pair, LANES * pair + LANES)
        q = q_ref[0, :, sl]
        zero = jnp.zeros_like(q)
        k = jnp.concatenate([r[0, :, sl] for r in k_refs], axis=0).astype(BF16)
        v = jnp.concatenate([r[0, :, sl] for r in v_refs], axis=0).astype(BF16)
        outs = []
        for half in (0, 1):
            qh = jnp.where(lo, q, zero) if half == 0 else jnp.where(lo, zero, q)
            s = lax.dot_general(qh, k, (((1,), (1,)), ((), ())), preferred_element_type=F32)
            s = jnp.where(valid, s + bias_ref[2 * pair + half], NEG_INF)
            m = jnp.max(s, axis=1, keepdims=True)
            pr = jnp.exp(s - m)
            den = jnp.sum(pr, axis=1, keepdims=True)
            outs.append(jnp.dot(pr.astype(BF16), v, preferred_element_type=F32) * (1.0 / den))
        o_ref[0, :, sl] = jnp.where(lo, outs[0], outs[1]).astype(o_ref.dtype)


def _band_keys(rows):
    return -(-(B_WIN + rows) // LANES) * LANES


def _band_bias(table, rows):
    lk = _band_keys(rows)
    w = rows + lk - 1
    d = np.concatenate([np.arange(lk), np.arange(-(rows - 1), 0)])
    idx = np.clip(B_WIN - d, -B_REL_CLIP, B_REL_CLIP) + B_REL_CLIP
    e = table.astype(F32)[:, idx]
    toep = jnp.tile(e, (1, rows))[:, :rows * (w - 1)].reshape(-1, rows, w - 1)[:, :, :lk]
    i = np.arange(rows)[:, None]
    j = np.arange(lk)[None, :]
    lo_edge = (i // CHUNK) * CHUNK
    inband = (j >= lo_edge) & (j < lo_edge + B_WIN + CHUNK)
    return jnp.where(jnp.asarray(inband)[None], toep, NEG_INF)


def _band(q, k, v, table, *, rows, off):
    b, t, w = q.shape
    nblk = _band_keys(rows) // rows
    shift = (B_WIN - off) // rows
    last = k.shape[1] // rows - 1
    bias = _band_bias(table, rows)
    kv_specs = [pl.BlockSpec((1, rows, w), lambda bb, cc, i=i: (bb, jnp.clip(cc + i - shift, 0, last), 0))
                for i in range(nblk)]
    kern = functools.partial(_band_kernel, nblk=nblk, rows=rows, thr0=B_WIN - off)
    return pl.pallas_call(
        kern,
        grid=(b, t // rows),
        in_specs=[pl.BlockSpec((1, rows, w), lambda bb, cc: (bb, cc, 0))] + kv_specs + kv_specs
                 + [pl.BlockSpec(bias.shape, lambda bb, cc: (0, 0, 0))],
        out_specs=pl.BlockSpec((1, rows, w), lambda bb, cc: (bb, cc, 0)),
        out_shape=jax.ShapeDtypeStruct(q.shape, BF16),
        compiler_params=_params(2),
        name="band_attn",
    )(q, *([k] * nblk), *([v] * nblk), bias)


def _gelu_tanh(x):
    return 0.5 * x * (1.0 + jnp.tanh(math.sqrt(2.0 / math.pi) * (x + 0.044715 * (x * x * x))))


def _rglru_kernel(dx_ref, dg_ref, buf_ref, h0_ref, cw_ref, cb_ref, wa_ref, ba_ref, wx_ref, bx_ref, lam_ref,
                  od_ref, nbuf_ref, hl_ref, tail_ref, hst_ref, a_s, b_s, h_s, *, tt):
    j = pl.program_id(1)
    nj = pl.num_programs(1)

    @pl.when(j == 0)
    def _():
        tail_ref[...] = jnp.zeros(tail_ref.shape, F32)
        tail_ref[5:8, :] = buf_ref[0]
        hst_ref[...] = h0_ref[0]

    x = dx_ref[0]
    xs = jnp.concatenate([tail_ref[...], x], axis=0)
    cw = cw_ref[...]
    u = cb_ref[...] + cw[3:4] * x
    for tap in range(D_CONV - 1):
        u = u + cw[tap:tap + 1] * xs[5 + tap:5 + tap + tt]
    tail_ref[...] = x[tt - 8:tt]

    ub = u.astype(BF16)
    r = _sigmoid(jnp.dot(ub, wa_ref[...], preferred_element_type=F32) + ba_ref[...])
    gate_i = _sigmoid(jnp.dot(ub, wx_ref[...], preferred_element_type=F32) + bx_ref[...])
    log_a = -RG_C * r * _softplus(-lam_ref[...])
    a_s[...] = jnp.exp(log_a)
    b_s[...] = jnp.sqrt(-_expm1(2.0 * log_a)) * (gate_i * u)

    def body(i, h):
        base = pl.multiple_of(i * 8, 8)
        for rr in range(8):
            h = a_s[pl.ds(base + rr, 1), :] * h + b_s[pl.ds(base + rr, 1), :]
            h_s[pl.ds(base + rr, 1), :] = h
        return h

    h_fin = lax.fori_loop(0, tt // 8, body, hst_ref[...])
    hst_ref[...] = h_fin
    od_ref[0] = (h_s[...] * _gelu_tanh(dg_ref[0])).astype(od_ref.dtype)

    @pl.when(j == nj - 1)
    def _():
        nbuf_ref[0] = tail_ref[8 - (D_CONV - 1):8, :]
        hl_ref[0] = h_fin


def _rglru(dx, dg, buf, h0, cw, cb, wa, ba, wx, bx, lam, *, tt):
    b, t, w = dx.shape
    const2 = lambda shape: pl.BlockSpec(shape, lambda bb, jj: (0, 0))
    return pl.pallas_call(
        functools.partial(_rglru_kernel, tt=tt),
        grid=(b, t // tt),
        in_specs=[pl.BlockSpec((1, tt, w), lambda bb, jj: (bb, jj, 0)),
                  pl.BlockSpec((1, tt, w), lambda bb, jj: (bb, jj, 0)),
                  pl.BlockSpec((1, D_CONV - 1, w), lambda bb, jj: (bb, 0, 0)),
                  pl.BlockSpec((1, 1, w), lambda bb, jj: (bb, 0, 0)),
                  const2((D_CONV, w)), const2((1, w)), const2((w, w)), const2((1, w)),
                  const2((w, w)), const2((1, w)), const2((1, w))],
        out_specs=[pl.BlockSpec((1, tt, w), lambda bb, jj: (bb, jj, 0)),
                   pl.BlockSpec((1, D_CONV - 1, w), lambda bb, jj: (bb, 0, 0)),
                   pl.BlockSpec((1, 1, w), lambda bb, jj: (bb, 0, 0))],
        out_shape=[jax.ShapeDtypeStruct((b, t, w), BF16),
                   jax.ShapeDtypeStruct((b, D_CONV - 1, w), F32),
                   jax.ShapeDtypeStruct((b, 1, w), F32)],
        scratch_shapes=[pltpu.VMEM((8, w), F32), pltpu.VMEM((1, w), F32),
                        pltpu.VMEM((tt, w), F32), pltpu.VMEM((tt, w), F32), pltpu.VMEM((tt, w), F32)],
        compiler_params=_params(2),
        name="rglru",
    )(dx, dg, buf, h0, cw, cb, wa, ba, wx, bx, lam)


def _tail_kernel(x_ref, o1_ref, o2_ref, wo_ref, g_ref, w1_ref, w3_ref, w2_ref, fg_ref, out_ref,
                 y_ref, h_ref, acc_ref, *, final_norm):
    j = pl.program_id(1)
    half = wo_ref.shape[0] // 2

    @pl.when(j == 0)
    def _():
        y = (x_ref[...]
             + jnp.dot(o1_ref[...], wo_ref[0:half, :], preferred_element_type=F32)
             + jnp.dot(o2_ref[...], wo_ref[half:, :], preferred_element_type=F32))
        y_ref[...] = y
        h_ref[...] = _rms(y, g_ref[...]).astype(BF16)
        acc_ref[...] = jnp.zeros(acc_ref.shape, F32)

    h = h_ref[...]
    a = jnp.dot(h, w1_ref[...], preferred_element_type=F32)
    g = jnp.dot(h, w3_ref[...], preferred_element_type=F32)
    act = (a * _sigmoid(a) * g).astype(BF16)
    acc_ref[...] += jnp.dot(act, w2_ref[...], preferred_element_type=F32)

    @pl.when(j == pl.num_programs(1) - 1)
    def _():
        y = y_ref[...] + acc_ref[...]
        if final_norm:
            y = _rms(y, fg_ref[...])
        out_ref[...] = y


def _tail(x, o1, o2, wo, g, w1, w3, w2, fg, *, final_norm, tm, hc):
    n = x.shape[0]
    hw = o1.shape[1]
    return pl.pallas_call(
        functools.partial(_tail_kernel, final_norm=final_norm),
        grid=(n // tm, FFN_HIDDEN // hc),
        in_specs=[pl.BlockSpec((tm, D_MODEL), lambda i, j: (i, 0)),
                  pl.BlockSpec((tm, hw), lambda i, j: (i, 0)),
                  pl.BlockSpec((tm, hw), lambda i, j: (i, 0)),
                  pl.BlockSpec(wo.shape, lambda i, j: (0, 0)),
                  pl.BlockSpec((1, D_MODEL), lambda i, j: (0, 0)),
                  pl.BlockSpec((D_MODEL, hc), lambda i, j: (0, j)),
                  pl.BlockSpec((D_MODEL, hc), lambda i, j: (0, j)),
                  pl.BlockSpec((hc, D_MODEL), lambda i, j: (j, 0)),
                  pl.BlockSpec((1, D_MODEL), lambda i, j: (0, 0))],
        out_specs=pl.BlockSpec((tm, D_MODEL), lambda i, j: (i, 0)),
        out_shape=jax.ShapeDtypeStruct((n, D_MODEL), F32),
        scratch_shapes=[pltpu.VMEM((tm, D_MODEL), F32), pltpu.VMEM((tm, D_MODEL), BF16),
                        pltpu.VMEM((tm, D_MODEL), F32)],
        compiler_params=_params(2, 56),
        name="tail_ffn",
    )(x, o1, o2, wo, g, w1, w3, w2, fg)


def _pad_time(x, front, back):
    return jnp.pad(x, ((0, 0), (front, back), (0, 0)))


def _block_diag(w):
    n, d, _ = w.shape
    out = jnp.zeros((n * d, n * d), w.dtype)
    for i in range(n):
        out = out.at[i * d:(i + 1) * d, i * d:(i + 1) * d].set(w[i])
    return out


def _pad_queries(q, tq):
    return _pad_time(q, 0, (-q.shape[1]) % tq)


def _layer_ab(x, past, wts, *, tm, tq, tk, rows):
    b, t, _ = x.shape
    n = b * t
    w_in, slopes, a_lambda, a_subln_g, b_rel_bias, g_mix = wts
    aq, ak, av, bq, bk, bv = _norm_proj(
        x.reshape(n, D_MODEL), g_mix, w_in, jnp.zeros((1, LANES), F32),
        [512] * 6, [BF16, F32, F32, BF16, F32, F32], [SCALE, 1.0, 1.0, SCALE, 1.0, 1.0], False, tm)
    r3 = lambda z: z.reshape(b, t, 512)
    aq, ak, av, bq, bk, bv = map(r3, (aq, ak, av, bq, bk, bv))
    if past is None:
        ka, va, q_off = ak, av, 0
        kb, vb, off = bk, bv, 0
        keep = min(B_WIN, t)
        nbk, nbv = bk[:, t - keep:], bv[:, t - keep:]
    else:
        cak, cav, cbk, cbv = past
        p_len = cak.shape[1]
        back = (-(p_len + t)) % tk
        ka = _pad_time(jnp.concatenate([cak.reshape(b, p_len, 512), ak], axis=1), 0, back)
        va = _pad_time(jnp.concatenate([cav.reshape(b, p_len, 512), av], axis=1), 0, back)
        q_off = p_len
        lb = cbk.shape[1]
        kb = jnp.concatenate([cbk.reshape(b, lb, 512), bk], axis=1)
        vb = jnp.concatenate([cbv.reshape(b, lb, 512), bv], axis=1)
        off = lb
        nbk, nbv = kb[:, t:], vb[:, t:]
    o_a = _flash("diff", _pad_queries(aq, tq), ka, _alibi_key_cols(slopes, ka.shape[1]), va,
                 slopes, a_lambda, a_subln_g.reshape(1, LANES),
                 tq=tq, tk=tk, q_off=q_off, lam_init=_lambda_init(0))[:, :t]
    o_b = _band(bq, kb, vb, b_rel_bias, rows=rows, off=off)
    state = (ak.reshape(b, t, A_HEADS, 2 * HEAD_DIM), av.reshape(b, t, A_HEADS, 2 * HEAD_DIM),
             nbk.reshape(b, -1, B_HEADS, HEAD_DIM), nbv.reshape(b, -1, B_HEADS, HEAD_DIM))
    return o_a.reshape(n, 512), o_b.reshape(n, 512), state


def _layer_cd(x, past, wts, *, tm, tq, tk, tt, tc):
    b, t, _ = x.shape
    n = b * t
    w_in, fb, cw, cb, wa, ba, wx, bx, lam, g_mix = wts
    cq, ck, cv, dx, dg, logf = _norm_proj(
        x.reshape(n, D_MODEL), g_mix, w_in, fb,
        [512] * 5 + [LANES], [BF16, F32, F32, F32, F32, F32], [SCALE, 1.0, 1.0, 1.0, 1.0, 1.0], True, tm)
    r3 = lambda z: z.reshape(b, t, z.shape[-1])
    cq, ck, cv, dx, dg, logf = map(r3, (cq, ck, cv, dx, dg, logf))
    if past is None:
        kc, vc, lf_all, q_off = ck, cv, logf, 0
        buf = jnp.zeros((b, D_CONV - 1, D_WIDTH), F32)
        h0 = jnp.zeros((b, 1, D_WIDTH), F32)
    else:
        cck, ccv, cclogf, buf, h0 = past
        p_len = cck.shape[1]
        back = (-(p_len + t)) % tk
        kc = _pad_time(jnp.concatenate([cck.reshape(b, p_len, 512), ck], axis=1), 0, back)
        vc = _pad_time(jnp.concatenate([ccv.reshape(b, p_len, 512), cv], axis=1), 0, back)
        q_off = p_len
        lf_all = _pad_time(jnp.concatenate(
            [jnp.pad(cclogf.astype(F32), ((0, 0), (0, 0), (0, LANES - C_HEADS))), logf], axis=1), 0, back)
        h0 = h0.reshape(b, 1, D_WIDTH)
    kx, f_rows = _fox_prep(lf_all, tc)
    f_rows = f_rows.reshape(b, C_HEADS // 2, 2, -1)
    dummy = jnp.zeros((1, LANES), F32)
    o_c = _flash("fox", _pad_queries(cq, tq), kc, kx, vc, f_rows, dummy, dummy,
                 tq=tq, tk=tk, q_off=q_off)[:, :t]
    o_d, nbuf, hl = _rglru(dx, dg, buf, h0, cw, cb, wa, ba, wx, bx, lam, tt=tt)
    state = (ck.reshape(b, t, C_HEADS, HEAD_DIM), cv.reshape(b, t, C_HEADS, HEAD_DIM),
             logf[:, :, :C_HEADS], nbuf, hl.reshape(b, D_WIDTH))
    return o_c.reshape(n, 512), o_d.reshape(n, 512), state


def kernel(x_prompt, x_sample, cache_a_k, cache_a_v, cache_b_k, cache_b_v, cache_c_k, cache_c_v, cache_c_logf, state_d_conv, state_d_h, norm_mix_g, norm_ffn_g, ab_w_in, ab_w_out, a_lambda, a_subln_g, b_rel_bias, cd_w_in, cd_w_out, c_f_bias, d_conv_w, d_conv_b, d_w_a, d_b_a, d_w_x, d_b_x, d_lambda, ffn_w1, ffn_w3, ffn_w2, final_g):
    slopes = jnp.asarray([2.0 ** (-8.0 * (h + 1) / A_HEADS) for h in range(A_HEADS)], F32)
    row = lambda z: z.reshape(1, -1).astype(F32)

    ab_wts = (ab_w_in.astype(BF16), slopes, a_lambda.astype(F32), a_subln_g.astype(F32), b_rel_bias,
              row(norm_mix_g[0]))
    w_cd = jnp.concatenate([cd_w_in[:, :1536], cd_w_in[:, 1544:], cd_w_in[:, 1536:1544],
                            jnp.zeros((D_MODEL, LANES - C_HEADS), cd_w_in.dtype)], axis=1).astype(BF16)
    fb = jnp.pad(c_f_bias.astype(F32), (0, LANES - C_HEADS)).reshape(1, LANES)
    cd_wts = (w_cd, fb, d_conv_w.astype(F32), row(d_conv_b), _block_diag(d_w_a).astype(BF16), row(d_b_a),
              _block_diag(d_w_x).astype(BF16), row(d_b_x), row(d_lambda), row(norm_mix_g[1]))
    wo = (ab_w_out.astype(BF16), cd_w_out.astype(BF16))
    w1, w3, w2 = ffn_w1.astype(BF16), ffn_w3.astype(BF16), ffn_w2.astype(BF16)
    fg = row(final_g)

    def trunk(x, past_ab, past_cd, cfg):
        b, t, _ = x.shape
        o_a, o_b, st_ab = _layer_ab(x, past_ab, ab_wts, tm=cfg["tm"], tq=cfg["tq"], tk=cfg["tk"],
                                    rows=cfg["rows"])
        y = _tail(x.reshape(b * t, D_MODEL), o_a, o_b, wo[0], row(norm_ffn_g[0]), w1[0], w3[0], w2[0], fg,
                  final_norm=False, tm=cfg["tm"], hc=cfg["hc"])
        o_c, o_d, st_cd = _layer_cd(y.reshape(b, t, D_MODEL), past_cd, cd_wts, tm=cfg["tm"], tq=cfg["tq"],
                                    tk=cfg["tk"], tt=cfg["tt"], tc=cfg["tc"])
        y = _tail(y, o_c, o_d, wo[1], row(norm_ffn_g[1]), w1[1], w3[1], w2[1], fg,
                  final_norm=True, tm=cfg["tm"], hc=cfg["hc"])
        return (y.reshape(b, t, D_MODEL),) + st_ab + st_cd

    prompt_cfg = dict(tm=512, tq=512, tk=512, rows=256, hc=1408, tt=512, tc=512)
    sample_cfg = dict(tm=512, tq=128, tk=512, rows=64, hc=1408, tt=64, tc=512)
    outp = trunk(x_prompt, None, None, prompt_cfg)
    outs = trunk(x_sample, (cache_a_k, cache_a_v, cache_b_k, cache_b_v),
                 (cache_c_k, cache_c_v, cache_c_logf, state_d_conv, state_d_h), sample_cfg)
    return (outp[0], outs[0]) + outp[1:] + outs[1:]
```

```python
import functools
import math

import numpy as np
import jax
import jax.numpy as jnp
from jax import lax
from jax.experimental import pallas as pl
from jax.experimental.pallas import tpu as pltpu

F32 = jnp.float32
BF16 = jnp.bfloat16

D_MODEL = 1024
CHUNK = 64
HEAD_DIM = 64
EPS = 1e-6
NEG_INF = -1e30
SCALE = HEAD_DIM ** -0.5
A_HEADS = 4
B_HEADS = 8
B_WIN = 512
B_REL_CLIP = 128
C_HEADS = 8
D_WIDTH = 512
D_CONV = 4
RG_C = 8.0
FFN_HIDDEN = 2816
LANES = 128
BF16_ROWS = 16
MIB = 1024 * 1024

EXT_ONES_K = 6
EXT_USED = 9


def _lambda_init(layer):
    return 0.8 - 0.6 * math.exp(-0.3 * layer)


def _params(n_axes, vmem_mib=48):
    return pltpu.CompilerParams(dimension_semantics=("arbitrary",) * n_axes,
                                vmem_limit_bytes=vmem_mib * MIB)


def _rms(x, g):
    return x * lax.rsqrt(jnp.mean(x * x, axis=-1, keepdims=True) + EPS) * g


def _sigmoid(x):
    return 1.0 / (1.0 + jnp.exp(-x))


def _expm1(x):
    u = jnp.exp(x)
    return jnp.where(u == 1.0, x, (u - 1.0) * x / jnp.log(jnp.where(u == 1.0, 2.0, u)))


def _softplus(x):
    return jnp.maximum(x, 0.0) + jnp.log1p(jnp.exp(-jnp.abs(x)))


def _split3(x):
    x1 = x.astype(BF16)
    r1 = x - x1.astype(F32)
    x2 = r1.astype(BF16)
    x3 = (r1 - x2.astype(F32)).astype(BF16)
    return x1, x2, x3


def _norm_proj_kernel(x_ref, g_ref, w_ref, fb_ref, *out_refs, widths, scales, logsig_last, t_segs):
    h = _rms(x_ref[...], g_ref[...]).astype(BF16)
    t_refs = out_refs[len(widths):]
    off = 0
    for idx, (o_ref, wd) in enumerate(zip(out_refs, widths)):
        y = jnp.dot(h, w_ref[:, off:off + wd], preferred_element_type=F32)
        if logsig_last and idx == len(widths) - 1:
            y = -_softplus(-(y + fb_ref[...]))
        if scales[idx] != 1.0:
            y = y * scales[idx]
        o_ref[...] = y.astype(o_ref.dtype)
        if idx in t_segs:
            t_refs[t_segs.index(idx)][0] = y.T.astype(BF16)
        off += wd


def _norm_proj(x, g, w, fb, widths, dtypes, scales, logsig_last, tm, t_segs=()):
    n = x.shape[0]
    kern = functools.partial(_norm_proj_kernel, widths=tuple(widths), scales=tuple(scales),
                             logsig_last=logsig_last, t_segs=tuple(t_segs))
    return pl.pallas_call(
        kern,
        grid=(n // tm,),
        in_specs=[pl.BlockSpec((tm, D_MODEL), lambda i: (i, 0)),
                  pl.BlockSpec((1, D_MODEL), lambda i: (0, 0)),
                  pl.BlockSpec(w.shape, lambda i: (0, 0)),
                  pl.BlockSpec((1, LANES), lambda i: (0, 0))],
        out_specs=[pl.BlockSpec((tm, wd), lambda i: (i, 0)) for wd in widths]
                  + [pl.BlockSpec((1, widths[s], tm), lambda i: (i, 0, 0)) for s in t_segs],
        out_shape=[jax.ShapeDtypeStruct((n, wd), dt) for wd, dt in zip(widths, dtypes)]
                  + [jax.ShapeDtypeStruct((n // tm, widths[s], tm), BF16) for s in t_segs],
        compiler_params=_params(1),
        name="norm_proj",
    )(x, g, w, fb)


def _fox_prep_kernel(x_ref, sel_ref, ones_ref, kx_ref, ft_ref, carry_ref, *, tt):
    @pl.when(pl.program_id(1) == 0)
    def _():
        carry_ref[...] = jnp.zeros(carry_ref.shape, F32)

    lane = lax.broadcasted_iota(jnp.int32, (1, LANES), 1)
    x = jnp.where(lane < C_HEADS, x_ref[0], 0.0)
    row = lax.broadcasted_iota(jnp.int32, (tt, tt), 0)
    col = lax.broadcasted_iota(jnp.int32, (tt, tt), 1)
    tri = jnp.where(col <= row, 1.0, 0.0).astype(BF16)
    x1, x2, x3 = _split3(x)
    f = (jnp.dot(tri, x1, preferred_element_type=F32) + jnp.dot(tri, x2, preferred_element_type=F32)
         + jnp.dot(tri, x3, preferred_element_type=F32)) + carry_ref[0:1, :]
    carry_ref[0:1, :] = f[tt - 1:tt, :]
    f1, f2, f3 = _split3(f)
    pieces = jnp.concatenate([f1, f2, f3], axis=1)
    kx_ref[0] = (ones_ref[...] - jnp.dot(pieces, sel_ref[...], preferred_element_type=F32)).astype(BF16)
    r = lax.broadcasted_iota(jnp.int32, (BF16_ROWS, LANES), 0)
    c = lax.broadcasted_iota(jnp.int32, (BF16_ROWS, LANES), 1)
    eye = jnp.where(r == c, 1.0, 0.0).astype(BF16)
    dn = (((1,), (1,)), ((), ()))
    ft = (lax.dot_general(eye, f1, dn, preferred_element_type=F32)
          + lax.dot_general(eye, f2, dn, preferred_element_type=F32)
          + lax.dot_general(eye, f3, dn, preferred_element_type=F32))
    ft_ref[0] = ft[0:C_HEADS, :]


def _fox_sel():
    sel = np.zeros((3 * LANES, C_HEADS // 2 * LANES), np.float32)
    ones = np.zeros((1, C_HEADS // 2 * LANES), np.float32)
    for pair in range(C_HEADS // 2):
        for half in (0, 1):
            for piece in range(3):
                sel[piece * LANES + 2 * pair + half, pair * LANES + 3 * half + piece] = 1.0
        ones[0, pair * LANES + EXT_ONES_K:pair * LANES + EXT_USED] = 1.0
    return jnp.asarray(sel, BF16), jnp.asarray(ones, F32)


def _fox_prep(logf, tt):
    b, t, _ = logf.shape
    sel, ones = _fox_sel()
    return pl.pallas_call(
        functools.partial(_fox_prep_kernel, tt=tt),
        grid=(b, t // tt),
        in_specs=[pl.BlockSpec((1, tt, LANES), lambda i, j: (i, j, 0)),
                  pl.BlockSpec(sel.shape, lambda i, j: (0, 0)),
                  pl.BlockSpec(ones.shape, lambda i, j: (0, 0))],
        out_specs=[pl.BlockSpec((1, tt, sel.shape[1]), lambda i, j: (i, j, 0)),
                   pl.BlockSpec((1, C_HEADS, tt), lambda i, j: (i, 0, j))],
        out_shape=[jax.ShapeDtypeStruct((b, t, sel.shape[1]), BF16),
                   jax.ShapeDtypeStruct((b, C_HEADS, t), F32)],
        scratch_shapes=[pltpu.VMEM((8, LANES), F32)],
        compiler_params=_params(2),
        name="fox_prep",
    )(logf, sel, ones)


def _flash_kernel(q_ref, k_ref, kx_ref, vt_ref, mb_ref, e1_ref, e2_ref, e3_ref, o_ref,
                  qt_ref, m_ref, acc_ref, s0_ref, s1_ref, p0_ref, p1_ref, al0_ref, al1_ref, mx0_ref, mx1_ref,
                  *, mode, tq, tk, q_off, lam_init):
    h = pl.program_id(1)
    q_start = q_off + pl.program_id(2) * tq
    row_lo = lax.broadcasted_iota(jnp.int32, (LANES, 1), 0) < HEAD_DIM

    qt = q_ref[0].astype(F32).T
    zero = jnp.zeros_like(qt)
    r16 = lax.broadcasted_iota(jnp.int32, (BF16_ROWS, 1), 0)
    for half in (0, 1):
        qt_ref[half, 0:LANES, :] = (jnp.where(row_lo, qt, zero) if half == 0
                                    else jnp.where(row_lo, zero, qt)).astype(BF16)
        if mode == "diff":
            qpos = q_start + lax.broadcasted_iota(jnp.int32, (1, tq), 1)
            b = -e1_ref[h] * qpos.astype(F32)
        else:
            b = e1_ref[0, 0][half:half + 1, :]
        b1, b2, b3 = (z.astype(F32) for z in _split3(b))
        own = (r16 >= 3 * half) & (r16 < 3 * half + 3)
        ext = jnp.where(own, 1.0, jnp.where(r16 == EXT_ONES_K, b1, jnp.where(
            r16 == EXT_ONES_K + 1, b2, jnp.where(r16 == EXT_ONES_K + 2, b3, 0.0))))
        qt_ref[half, LANES:LANES + BF16_ROWS, :] = ext.astype(BF16)
        qt_ref[half, LANES + BF16_ROWS:, :] = jnp.zeros((LANES - BF16_ROWS, tq), BF16)
    m_ref[...] = jnp.full(m_ref.shape, NEG_INF, F32)
    acc_ref[...] = jnp.zeros(acc_ref.shape, F32)
    p1_ref[...] = jnp.zeros(p1_ref.shape, BF16)
    al1_ref[...] = jnp.ones(al1_ref.shape, F32)

    def scores(t, buf):
        s_ref, mx_ref = buf[0], buf[1]
        ks = pl.multiple_of(t * tk, tk)
        kaug = jnp.concatenate([k_ref[0, pl.ds(ks, tk), :].astype(BF16), kx_ref[0, pl.ds(ks, tk), :]], axis=1)
        for half in (0, 1):
            s = jnp.dot(kaug, qt_ref[half], preferred_element_type=F32)
            s_ref[half] = s
            mx_ref[half] = jnp.max(s, axis=0, keepdims=True)

    def softmax(t, buf, masked):
        s_ref, mx_ref, p_ref, al_ref = buf
        for half in (0, 1):
            s = s_ref[half]
            if masked:
                s = s + mb_ref[0]
            m_prev = m_ref[half]
            m_new = jnp.maximum(m_prev, jnp.max(s, axis=0, keepdims=True) if masked else mx_ref[half])
            al_ref[half] = jnp.exp(m_prev - m_new)
            p_ref[half] = jnp.exp(s - m_new).astype(BF16)
            m_ref[half] = m_new

    def values(t, buf):
        p_ref, al_ref = buf[2], buf[3]
        vt = jnp.concatenate([vt_ref[0, jnp.maximum(t, 0)], jnp.ones((BF16_ROWS, tk), BF16)], axis=0)
        for half in (0, 1):
            acc_ref[half] = al_ref[half] * acc_ref[half] + jnp.dot(vt, p_ref[half], preferred_element_type=F32)

    n_int = lax.div(q_start, tk)
    n_pairs = lax.div(n_int, 2)

    even = (s0_ref, mx0_ref, p0_ref, al0_ref)
    odd = (s1_ref, mx1_ref, p1_ref, al1_ref)

    def pair_body(i, carry):
        t = 2 * i
        values(t - 1, odd)
        scores(t + 1, odd)
        softmax(t, even, False)
        values(t, even)
        scores(t + 2, even)
        softmax(t + 1, odd, False)
        return carry

    scores(0, even)
    lax.fori_loop(0, n_pairs, pair_body, 0)
    t0 = 2 * n_pairs

    @pl.when(n_int > t0)
    def _():
        values(t0 - 1, odd)
        scores(t0 + 1, odd)
        softmax(t0, even, False)
        values(t0, even)
        softmax(t0 + 1, odd, True)
        values(t0 + 1, odd)

    @pl.when(n_int == t0)
    def _():
        values(t0 - 1, odd)
        softmax(t0, even, True)
        values(t0, even)

    a0 = acc_ref[0]
    a1 = acc_ref[1]
    o0 = a0[0:LANES] * (1.0 / a0[LANES:LANES + 1])
    o1 = a1[0:LANES] * (1.0 / a1[LANES:LANES + 1])
    if mode == "diff":
        a = e2_ref[...]
        lam = (jnp.exp(jnp.sum(a[0:1] * a[1:2], axis=1, keepdims=True))
               - jnp.exp(jnp.sum(a[2:3] * a[3:4], axis=1, keepdims=True)) + lam_init)
        o = (o0 - lam * o1).T
        o_ref[0] = (_rms(o, e3_ref[...]) * (1.0 - lam_init)).astype(o_ref.dtype)
    else:
        o_ref[0] = jnp.where(row_lo, o0, o1).T.astype(o_ref.dtype)


def _flash(mode, q, k, kx, vt, e1, e2, e3, *, tq, tk, q_off, lam_init=0.0):
    b, t_q, _ = q.shape
    t_k = k.shape[1]
    n_col = q.shape[2] // LANES
    nt = t_k // tk
    assert tk % tq == 0 and q_off % tq == 0 and t_k >= q_off + t_q and (tq == tk or t_q == tq)
    kpos = jnp.arange(tk, dtype=jnp.int32)[:, None]
    qpos = q_off % tk + jnp.arange(tq, dtype=jnp.int32)[None, :]
    if mode == "diff":
        corr = -2.0 * e1[:, None, None] * jnp.maximum(kpos - qpos, 0).astype(F32)[None]
        mask_bias = jnp.where(((kpos >> 6) <= (qpos >> 6))[None], corr, NEG_INF)
        mb_spec = pl.BlockSpec((1, tk, tq), lambda bb, hh, qq: (hh, 0, 0))
    else:
        mask_bias = jnp.where(kpos <= qpos, 0.0, NEG_INF).astype(F32)[None]
        mb_spec = pl.BlockSpec((1, tk, tq), lambda bb, hh, qq: (0, 0, 0))
    const2 = lambda shape: pl.BlockSpec(shape, lambda bb, hh, qq: (0, 0))
    if mode == "diff":
        kx_spec = pl.BlockSpec((1, t_k, LANES), lambda bb, hh, qq: (hh, 0, 0))
        e1_spec = pl.BlockSpec(memory_space=pltpu.SMEM)
    else:
        q_blk0 = q_off // tq
        kx_spec = pl.BlockSpec((1, t_k, LANES), lambda bb, hh, qq: (bb, 0, hh))
        e1_spec = pl.BlockSpec((1, 1, 2, tq), lambda bb, hh, qq: (bb, hh, 0, q_blk0 + qq))
    kern = functools.partial(_flash_kernel, mode=mode, tq=tq, tk=tk, q_off=q_off, lam_init=lam_init)
    return pl.pallas_call(
        kern,
        grid=(b, n_col, t_q // tq),
        in_specs=[pl.BlockSpec((1, tq, LANES), lambda bb, hh, qq: (bb, qq, hh)),
                  pl.BlockSpec((1, t_k, LANES), lambda bb, hh, qq: (bb, 0, hh)),
                  kx_spec,
                  pl.BlockSpec((1, nt, LANES, tk), lambda bb, hh, qq: (bb, 0, hh, 0)),
                  mb_spec, e1_spec, const2(e2.shape), const2(e3.shape)],
        out_specs=pl.BlockSpec((1, tq, LANES), lambda bb, hh, qq: (bb, qq, hh)),
        out_shape=jax.ShapeDtypeStruct(q.shape, BF16),
        scratch_shapes=[pltpu.VMEM((2, 2 * LANES, tq), BF16),
                        pltpu.VMEM((2, 1, tq), F32),
                        pltpu.VMEM((2, LANES + BF16_ROWS, tq), F32),
                        pltpu.VMEM((2, tk, tq), F32), pltpu.VMEM((2, tk, tq), F32),
                        pltpu.VMEM((2, tk, tq), BF16), pltpu.VMEM((2, tk, tq), BF16),
                        pltpu.VMEM((2, 1, tq), F32), pltpu.VMEM((2, 1, tq), F32),
                        pltpu.VMEM((2, 1, tq), F32), pltpu.VMEM((2, 1, tq), F32)],
        compiler_params=_params(3),
        name="flash_" + mode,
    )(q, k, kx, vt, mask_bias, e1, e2, e3)


def _alibi_key_cols(slopes, t_k):
    pos = jnp.arange(t_k, dtype=jnp.int32)
    a1 = slopes[:, None] * ((pos >> 7) << 7).astype(F32)[None, :]
    a2 = slopes[:, None] * (pos & 127).astype(F32)[None, :]
    zero = jnp.zeros_like(a1)
    one = jnp.ones_like(a1)
    cols = jnp.stack([a1, a2, zero, a1, a2, zero, one, one, one], axis=-1)
    return jnp.pad(cols, ((0, 0), (0, 0), (0, LANES - EXT_USED))).astype(BF16)


def _band_kernel(q_ref, *rest, nblk, rows, thr0):
    k_refs = rest[:nblk]
    v_refs = rest[nblk:2 * nblk]
    bias_ref = rest[2 * nblk]
    o_ref = rest[2 * nblk + 1]
    c = pl.program_id(1)
    lk = nblk * rows
    valid = lax.broadcasted_iota(jnp.int32, (1, lk), 1) >= (thr0 - c * rows)
    lo = lax.broadcasted_iota(jnp.int32, (1, LANES), 1) < HEAD_DIM
    for pair in range(B_HEADS // 2):
        sl = slice(LANES * pair, LANES * pair + LANES)
        q = q_ref[0, :, sl]
        zero = jnp.zeros_like(q)
        k = jnp.concatenate([r[0, :, sl] for r in k_refs], axis=0).astype(BF16)
        v = jnp.concatenate([r[0, :, sl] for r in v_refs], axis=0).astype(BF16)
        outs = []
        for half in (0, 1):
            qh = jnp.where(lo, q, zero) if half == 0 else jnp.where(lo, zero, q)
            s = lax.dot_general(qh, k, (((1,), (1,)), ((), ())), preferred_element_type=F32)
            s = jnp.where(valid, s + bias_ref[2 * pair + half], NEG_INF)
            m = jnp.max(s, axis=1, keepdims=True)
            pr = jnp.exp(s - m)
            den = jnp.sum(pr, axis=1, keepdims=True)
            outs.append(jnp.dot(pr.astype(BF16), v, preferred_element_type=F32) * (1.0 / den))
        o_ref[0, :, sl] = jnp.where(lo, outs[0], outs[1]).astype(o_ref.dtype)


def _band_keys(rows):
    return -(-(B_WIN + rows) // LANES) * LANES


def _band_bias(table, rows):
    lk = _band_keys(rows)
    w = rows + lk - 1
    d = np.concatenate([np.arange(lk), np.arange(-(rows - 1), 0)])
    idx = np.clip(B_WIN - d, -B_REL_CLIP, B_REL_CLIP) + B_REL_CLIP
    e = table.astype(F32)[:, idx]
    toep = jnp.tile(e, (1, rows))[:, :rows * (w - 1)].reshape(-1, rows, w - 1)[:, :, :lk]
    i = np.arange(rows)[:, None]
    j = np.arange(lk)[None, :]
    lo_edge = (i // CHUNK) * CHUNK
    inband = (j >= lo_edge) & (j < lo_edge + B_WIN + CHUNK)
    return jnp.where(jnp.asarray(inband)[None], toep, NEG_INF)


def _band(q, k, v, table, *, rows, off):
    b, t, w = q.shape
    nblk = _band_keys(rows) // rows
    shift = (B_WIN - off) // rows
    last = k.shape[1] // rows - 1
    bias = _band_bias(table, rows)
    kv_specs = [pl.BlockSpec((1, rows, w), lambda bb, cc, i=i: (bb, jnp.clip(cc + i - shift, 0, last), 0))
                for i in range(nblk)]
    kern = functools.partial(_band_kernel, nblk=nblk, rows=rows, thr0=B_WIN - off)
    return pl.pallas_call(
        kern,
        grid=(b, t // rows),
        in_specs=[pl.BlockSpec((1, rows, w), lambda bb, cc: (bb, cc, 0))] + kv_specs + kv_specs
                 + [pl.BlockSpec(bias.shape, lambda bb, cc: (0, 0, 0))],
        out_specs=pl.BlockSpec((1, rows, w), lambda bb, cc: (bb, cc, 0)),
        out_shape=jax.ShapeDtypeStruct(q.shape, BF16),
        compiler_params=_params(2),
        name="band_attn",
    )(q, *([k] * nblk), *([v] * nblk), bias)


def _gelu_tanh(x):
    return 0.5 * x * (1.0 + jnp.tanh(math.sqrt(2.0 / math.pi) * (x + 0.044715 * (x * x * x))))


def _rglru_kernel(dx_ref, dg_ref, buf_ref, h0_ref, cw_ref, cb_ref, wa_ref, ba_ref, wx_ref, bx_ref, lam_ref,
                  od_ref, nbuf_ref, hl_ref, tail_ref, hst_ref, a_s, b_s, h_s, *, tt):
    j = pl.program_id(1)
    nj = pl.num_programs(1)

    @pl.when(j == 0)
    def _():
        tail_ref[...] = jnp.zeros(tail_ref.shape, F32)
        tail_ref[5:8, :] = buf_ref[0]
        hst_ref[...] = h0_ref[0]

    x = dx_ref[0]
    xs = jnp.concatenate([tail_ref[...], x], axis=0)
    cw = cw_ref[...]
    u = cb_ref[...] + cw[3:4] * x
    for tap in range(D_CONV - 1):
        u = u + cw[tap:tap + 1] * xs[5 + tap:5 + tap + tt]
    tail_ref[...] = x[tt - 8:tt]

    ub = u.astype(BF16)
    r = _sigmoid(jnp.dot(ub, wa_ref[...], preferred_element_type=F32) + ba_ref[...])
    gate_i = _sigmoid(jnp.dot(ub, wx_ref[...], preferred_element_type=F32) + bx_ref[...])
    log_a = -RG_C * r * _softplus(-lam_ref[...])
    a_s[...] = jnp.exp(log_a)
    b_s[...] = jnp.sqrt(-_expm1(2.0 * log_a)) * (gate_i * u)

    def body(i, h):
        base = pl.multiple_of(i * 8, 8)
        for rr in range(8):
            h = a_s[pl.ds(base + rr, 1), :] * h + b_s[pl.ds(base + rr, 1), :]
            h_s[pl.ds(base + rr, 1), :] = h
        return h

    h_fin = lax.fori_loop(0, tt // 8, body, hst_ref[...])
    hst_ref[...] = h_fin
    od_ref[0] = (h_s[...] * _gelu_tanh(dg_ref[0])).astype(od_ref.dtype)

    @pl.when(j == nj - 1)
    def _():
        nbuf_ref[0] = tail_ref[8 - (D_CONV - 1):8, :]
        hl_ref[0] = h_fin


def _rglru(dx, dg, buf, h0, cw, cb, wa, ba, wx, bx, lam, *, tt):
    b, t, w = dx.shape
    const2 = lambda shape: pl.BlockSpec(shape, lambda bb, jj: (0, 0))
    return pl.pallas_call(
        functools.partial(_rglru_kernel, tt=tt),
        grid=(b, t // tt),
        in_specs=[pl.BlockSpec((1, tt, w), lambda bb, jj: (bb, jj, 0)),
                  pl.BlockSpec((1, tt, w), lambda bb, jj: (bb, jj, 0)),
                  pl.BlockSpec((1, D_CONV - 1, w), lambda bb, jj: (bb, 0, 0)),
                  pl.BlockSpec((1, 1, w), lambda bb, jj: (bb, 0, 0)),
                  const2((D_CONV, w)), const2((1, w)), const2((w, w)), const2((1, w)),
                  const2((w, w)), const2((1, w)), const2((1, w))],
        out_specs=[pl.BlockSpec((1, tt, w), lambda bb, jj: (bb, jj, 0)),
                   pl.BlockSpec((1, D_CONV - 1, w), lambda bb, jj: (bb, 0, 0)),
                   pl.BlockSpec((1, 1, w), lambda bb, jj: (bb, 0, 0))],
        out_shape=[jax.ShapeDtypeStruct((b, t, w), BF16),
                   jax.ShapeDtypeStruct((b, D_CONV - 1, w), F32),
                   jax.ShapeDtypeStruct((b, 1, w), F32)],
        scratch_shapes=[pltpu.VMEM((8, w), F32), pltpu.VMEM((1, w), F32),
                        pltpu.VMEM((tt, w), F32), pltpu.VMEM((tt, w), F32), pltpu.VMEM((tt, w), F32)],
        compiler_params=_params(2),
        name="rglru",
    )(dx, dg, buf, h0, cw, cb, wa, ba, wx, bx, lam)


def _tail_kernel(x_ref, o1_ref, o2_ref, wo_ref, g_ref, w1_ref, w3_ref, w2_ref, fg_ref, out_ref,
                 y_ref, h_ref, acc_ref, *, final_norm):
    j = pl.program_id(1)
    half = wo_ref.shape[0] // 2

    @pl.when(j == 0)
    def _():
        y = (x_ref[...]
             + jnp.dot(o1_ref[...], wo_ref[0:half, :], preferred_element_type=F32)
             + jnp.dot(o2_ref[...], wo_ref[half:, :], preferred_element_type=F32))
        y_ref[...] = y
        h_ref[...] = _rms(y, g_ref[...]).astype(BF16)
        acc_ref[...] = jnp.zeros(acc_ref.shape, F32)

    h = h_ref[...]
    a = jnp.dot(h, w1_ref[...], preferred_element_type=F32)
    g = jnp.dot(h, w3_ref[...], preferred_element_type=F32)
    act = (a * _sigmoid(a) * g).astype(BF16)
    acc_ref[...] += jnp.dot(act, w2_ref[...], preferred_element_type=F32)

    @pl.when(j == pl.num_programs(1) - 1)
    def _():
        y = y_ref[...] + acc_ref[...]
        if final_norm:
            y = _rms(y, fg_ref[...])
        out_ref[...] = y


def _tail(x, o1, o2, wo, g, w1, w3, w2, fg, *, final_norm, tm, hc):
    n = x.shape[0]
    hw = o1.shape[1]
    return pl.pallas_call(
        functools.partial(_tail_kernel, final_norm=final_norm),
        grid=(n // tm, FFN_HIDDEN // hc),
        in_specs=[pl.BlockSpec((tm, D_MODEL), lambda i, j: (i, 0)),
                  pl.BlockSpec((tm, hw), lambda i, j: (i, 0)),
                  pl.BlockSpec((tm, hw), lambda i, j: (i, 0)),
                  pl.BlockSpec(wo.shape, lambda i, j: (0, 0)),
                  pl.BlockSpec((1, D_MODEL), lambda i, j: (0, 0)),
                  pl.BlockSpec((D_MODEL, hc), lambda i, j: (0, j)),
                  pl.BlockSpec((D_MODEL, hc), lambda i, j: (0, j)),
                  pl.BlockSpec((hc, D_MODEL), lambda i, j: (j, 0)),
                  pl.BlockSpec((1, D_MODEL), lambda i, j: (0, 0))],
        out_specs=pl.BlockSpec((tm, D_MODEL), lambda i, j: (i, 0)),
        out_shape=jax.ShapeDtypeStruct((n, D_MODEL), F32),
        scratch_shapes=[pltpu.VMEM((tm, D_MODEL), F32), pltpu.VMEM((tm, D_MODEL), BF16),
                        pltpu.VMEM((tm, D_MODEL), F32)],
        compiler_params=_params(2, 56),
        name="tail_ffn",
    )(x, o1, o2, wo, g, w1, w3, w2, fg)


def _pad_time(x, front, back):
    return jnp.pad(x, ((0, 0), (front, back), (0, 0)))


def _block_diag(w):
    n, d, _ = w.shape
    out = jnp.zeros((n * d, n * d), w.dtype)
    for i in range(n):
        out = out.at[i * d:(i + 1) * d, i * d:(i + 1) * d].set(w[i])
    return out


def _pad_queries(q, tq):
    return _pad_time(q, 0, (-q.shape[1]) % tq)


def _vt_tiles(v, tile):
    b, t_k, w = v.shape
    return jnp.swapaxes(v.reshape(b, t_k // tile, tile, w), 2, 3).astype(BF16)


def _layer_ab(x, past, wts, *, tm, tq, tk, rows):
    b, t, _ = x.shape
    n = b * t
    w_in, slopes, a_lambda, a_subln_g, b_rel_bias, g_mix = wts
    prompt = past is None
    assert not prompt or tm == tk
    aq, ak, av, bq, bk, bv, *vts = _norm_proj(
        x.reshape(n, D_MODEL), g_mix, w_in, jnp.zeros((1, LANES), F32),
        [512] * 6, [BF16, F32, F32, BF16, F32, F32], [SCALE, 1.0, 1.0, SCALE, 1.0, 1.0], False, tm,
        t_segs=(2,) if prompt else ())
    r3 = lambda z: z.reshape(b, t, 512)
    aq, ak, av, bq, bk, bv = map(r3, (aq, ak, av, bq, bk, bv))
    if prompt:
        ka, q_off = ak, 0
        vta = vts[0].reshape(b, t // tm, 512, tm)
        kb, vb, off = bk, bv, 0
        keep = min(B_WIN, t)
        nbk, nbv = bk[:, t - keep:], bv[:, t - keep:]
    else:
        cak, cav, cbk, cbv = past
        p_len = cak.shape[1]
        back = (-(p_len + t)) % tk
        ka = _pad_time(jnp.concatenate([cak.reshape(b, p_len, 512), ak], axis=1), 0, back)
        vta = _vt_tiles(_pad_time(jnp.concatenate([cav.reshape(b, p_len, 512), av], axis=1), 0, back), tk)
        q_off = p_len
        lb = cbk.shape[1]
        kb = jnp.concatenate([cbk.reshape(b, lb, 512), bk], axis=1)
        vb = jnp.concatenate([cbv.reshape(b, lb, 512), bv], axis=1)
        off = lb
        nbk, nbv = kb[:, t:], vb[:, t:]
    o_a = _flash("diff", _pad_queries(aq, tq), ka, _alibi_key_cols(slopes, ka.shape[1]), vta,
                 slopes, a_lambda, a_subln_g.reshape(1, LANES),
                 tq=tq, tk=tk, q_off=q_off, lam_init=_lambda_init(0))[:, :t]
    o_b = _band(bq, kb, vb, b_rel_bias, rows=rows, off=off)
    state = (ak.reshape(b, t, A_HEADS, 2 * HEAD_DIM), av.reshape(b, t, A_HEADS, 2 * HEAD_DIM),
             nbk.reshape(b, -1, B_HEADS, HEAD_DIM), nbv.reshape(b, -1, B_HEADS, HEAD_DIM))
    return o_a.reshape(n, 512), o_b.reshape(n, 512), state


def _layer_cd(x, past, wts, *, tm, tq, tk, tt, tc):
    b, t, _ = x.shape
    n = b * t
    w_in, fb, cw, cb, wa, ba, wx, bx, lam, g_mix = wts
    prompt = past is None
    assert not prompt or tm == tk
    cq, ck, cv, dx, dg, logf, *vts = _norm_proj(
        x.reshape(n, D_MODEL), g_mix, w_in, fb,
        [512] * 5 + [LANES], [BF16, F32, F32, F32, F32, F32], [SCALE, 1.0, 1.0, 1.0, 1.0, 1.0], True, tm,
        t_segs=(2,) if prompt else ())
    r3 = lambda z: z.reshape(b, t, z.shape[-1])
    cq, ck, cv, dx, dg, logf = map(r3, (cq, ck, cv, dx, dg, logf))
    if prompt:
        kc, lf_all, q_off = ck, logf, 0
        vtc = vts[0].reshape(b, t // tm, 512, tm)
        buf = jnp.zeros((b, D_CONV - 1, D_WIDTH), F32)
        h0 = jnp.zeros((b, 1, D_WIDTH), F32)
    else:
        cck, ccv, cclogf, buf, h0 = past
        p_len = cck.shape[1]
        back = (-(p_len + t)) % tk
        kc = _pad_time(jnp.concatenate([cck.reshape(b, p_len, 512), ck], axis=1), 0, back)
        vtc = _vt_tiles(_pad_time(jnp.concatenate([ccv.reshape(b, p_len, 512), cv], axis=1), 0, back), tk)
        q_off = p_len
        lf_all = _pad_time(jnp.concatenate(
            [jnp.pad(cclogf.astype(F32), ((0, 0), (0, 0), (0, LANES - C_HEADS))), logf], axis=1), 0, back)
        h0 = h0.reshape(b, 1, D_WIDTH)
    kx, f_rows = _fox_prep(lf_all, tc)
    f_rows = f_rows.reshape(b, C_HEADS // 2, 2, -1)
    dummy = jnp.zeros((1, LANES), F32)
    o_c = _flash("fox", _pad_queries(cq, tq), kc, kx, vtc, f_rows, dummy, dummy,
                 tq=tq, tk=tk, q_off=q_off)[:, :t]
    o_d, nbuf, hl = _rglru(dx, dg, buf, h0, cw, cb, wa, ba, wx, bx, lam, tt=tt)
    state = (ck.reshape(b, t, C_HEADS, HEAD_DIM), cv.reshape(b, t, C_HEADS, HEAD_DIM),
             logf[:, :, :C_HEADS], nbuf, hl.reshape(b, D_WIDTH))
    return o_c.reshape(n, 512), o_d.reshape(n, 512), state


def kernel(x_prompt, x_sample, cache_a_k, cache_a_v, cache_b_k, cache_b_v, cache_c_k, cache_c_v, cache_c_logf, state_d_conv, state_d_h, norm_mix_g, norm_ffn_g, ab_w_in, ab_w_out, a_lambda, a_subln_g, b_rel_bias, cd_w_in, cd_w_out, c_f_bias, d_conv_w, d_conv_b, d_w_a, d_b_a, d_w_x, d_b_x, d_lambda, ffn_w1, ffn_w3, ffn_w2, final_g):
    slopes = jnp.asarray([2.0 ** (-8.0 * (h + 1) / A_HEADS) for h in range(A_HEADS)], F32)
    row = lambda z: z.reshape(1, -1).astype(F32)

    ab_wts = (ab_w_in.astype(BF16), slopes, a_lambda.astype(F32), a_subln_g.astype(F32), b_rel_bias,
              row(norm_mix_g[0]))
    w_cd = jnp.concatenate([cd_w_in[:, :1536], cd_w_in[:, 1544:], cd_w_in[:, 1536:1544],
                            jnp.zeros((D_MODEL, LANES - C_HEADS), cd_w_in.dtype)], axis=1).astype(BF16)
    fb = jnp.pad(c_f_bias.astype(F32), (0, LANES - C_HEADS)).reshape(1, LANES)
    cd_wts = (w_cd, fb, d_conv_w.astype(F32), row(d_conv_b), _block_diag(d_w_a).astype(BF16), row(d_b_a),
              _block_diag(d_w_x).astype(BF16), row(d_b_x), row(d_lambda), row(norm_mix_g[1]))
    wo = (ab_w_out.astype(BF16), cd_w_out.astype(BF16))
    w1, w3, w2 = ffn_w1.astype(BF16), ffn_w3.astype(BF16), ffn_w2.astype(BF16)
    fg = row(final_g)

    def trunk(x, past_ab, past_cd, cfg):
        b, t, _ = x.shape
        o_a, o_b, st_ab = _layer_ab(x, past_ab, ab_wts, tm=cfg["tm"], tq=cfg["tq"], tk=cfg["tk"],
                                    rows=cfg["rows"])
        y = _tail(x.reshape(b * t, D_MODEL), o_a, o_b, wo[0], row(norm_ffn_g[0]), w1[0], w3[0], w2[0], fg,
                  final_norm=False, tm=cfg["tm"], hc=cfg["hc"])
        o_c, o_d, st_cd = _layer_cd(y.reshape(b, t, D_MODEL), past_cd, cd_wts, tm=cfg["tm"], tq=cfg["tq"],
                                    tk=cfg["tk"], tt=cfg["tt"], tc=cfg["tc"])
        y = _tail(y, o_c, o_d, wo[1], row(norm_ffn_g[1]), w1[1], w3[1], w2[1], fg,
                  final_norm=True, tm=cfg["tm"], hc=cfg["hc"])
        return (y.reshape(b, t, D_MODEL),) + st_ab + st_cd

    prompt_cfg = dict(tm=512, tq=512, tk=512, rows=256, hc=1408, tt=512, tc=512)
    sample_cfg = dict(tm=512, tq=128, tk=512, rows=64, hc=1408, tt=64, tc=512)
    outp = trunk(x_prompt, None, None, prompt_cfg)
    outs = trunk(x_sample, (cache_a_k, cache_a_v, cache_b_k, cache_b_v),
                 (cache_c_k, cache_c_v, cache_c_logf, state_d_conv, state_d_h), sample_cfg)
    return (outp[0], outs[0]) + outp[1:] + outs[1:]
```

```python
import functools
import math

import numpy as np
import jax
import jax.numpy as jnp
from jax import lax
from jax.experimental import pallas as pl
from jax.experimental.pallas import tpu as pltpu

F32 = jnp.float32
BF16 = jnp.bfloat16

D_MODEL = 1024
CHUNK = 64
HEAD_DIM = 64
EPS = 1e-6
NEG_INF = -1e30
SCALE = HEAD_DIM ** -0.5
A_HEADS = 4
B_HEADS = 8
B_WIN = 512
B_REL_CLIP = 128
C_HEADS = 8
D_WIDTH = 512
D_CONV = 4
RG_C = 8.0
FFN_HIDDEN = 2816
LANES = 128
BF16_ROWS = 16
MIB = 1024 * 1024

EXT_ONES_K = 6
EXT_USED = 9


def _lambda_init(layer):
    return 0.8 - 0.6 * math.exp(-0.3 * layer)


def _params(n_axes, vmem_mib=48):
    return pltpu.CompilerParams(dimension_semantics=("arbitrary",) * n_axes,
                                vmem_limit_bytes=vmem_mib * MIB)


def _rms(x, g):
    return x * lax.rsqrt(jnp.mean(x * x, axis=-1, keepdims=True) + EPS) * g


def _sigmoid(x):
    return 1.0 / (1.0 + jnp.exp(-x))


def _expm1(x):
    u = jnp.exp(x)
    return jnp.where(u == 1.0, x, (u - 1.0) * x / jnp.log(jnp.where(u == 1.0, 2.0, u)))


def _softplus(x):
    return jnp.maximum(x, 0.0) + jnp.log1p(jnp.exp(-jnp.abs(x)))


def _split3(x):
    x1 = x.astype(BF16)
    r1 = x - x1.astype(F32)
    x2 = r1.astype(BF16)
    x3 = (r1 - x2.astype(F32)).astype(BF16)
    return x1, x2, x3


def _norm_proj_kernel(x_ref, g_ref, w_ref, fb_ref, *out_refs, widths, scales, logsig_last, t_segs):
    h = _rms(x_ref[...], g_ref[...]).astype(BF16)
    t_refs = out_refs[len(widths):]
    off = 0
    for idx, (o_ref, wd) in enumerate(zip(out_refs, widths)):
        y = jnp.dot(h, w_ref[:, off:off + wd], preferred_element_type=F32)
        if logsig_last and idx == len(widths) - 1:
            y = -_softplus(-(y + fb_ref[...]))
        if scales[idx] != 1.0:
            y = y * scales[idx]
        o_ref[...] = y.astype(o_ref.dtype)
        if idx in t_segs:
            t_refs[t_segs.index(idx)][0] = y.T.astype(BF16)
        off += wd


def _norm_proj(x, g, w, fb, widths, dtypes, scales, logsig_last, tm, t_segs=()):
    n = x.shape[0]
    kern = functools.partial(_norm_proj_kernel, widths=tuple(widths), scales=tuple(scales),
                             logsig_last=logsig_last, t_segs=tuple(t_segs))
    return pl.pallas_call(
        kern,
        grid=(n // tm,),
        in_specs=[pl.BlockSpec((tm, D_MODEL), lambda i: (i, 0)),
                  pl.BlockSpec((1, D_MODEL), lambda i: (0, 0)),
                  pl.BlockSpec(w.shape, lambda i: (0, 0)),
                  pl.BlockSpec((1, LANES), lambda i: (0, 0))],
        out_specs=[pl.BlockSpec((tm, wd), lambda i: (i, 0)) for wd in widths]
                  + [pl.BlockSpec((1, widths[s], tm), lambda i: (i, 0, 0)) for s in t_segs],
        out_shape=[jax.ShapeDtypeStruct((n, wd), dt) for wd, dt in zip(widths, dtypes)]
                  + [jax.ShapeDtypeStruct((n // tm, widths[s], tm), BF16) for s in t_segs],
        compiler_params=_params(1),
        name="norm_proj",
    )(x, g, w, fb)


def _fox_prep_kernel(x_ref, sel_ref, ones_ref, kx_ref, ft_ref, carry_ref, *, tt):
    @pl.when(pl.program_id(1) == 0)
    def _():
        carry_ref[...] = jnp.zeros(carry_ref.shape, F32)

    lane = lax.broadcasted_iota(jnp.int32, (1, LANES), 1)
    x = jnp.where(lane < C_HEADS, x_ref[0], 0.0)
    row = lax.broadcasted_iota(jnp.int32, (tt, tt), 0)
    col = lax.broadcasted_iota(jnp.int32, (tt, tt), 1)
    tri = jnp.where(col <= row, 1.0, 0.0).astype(BF16)
    x1, x2, x3 = _split3(x)
    f = (jnp.dot(tri, x1, preferred_element_type=F32) + jnp.dot(tri, x2, preferred_element_type=F32)
         + jnp.dot(tri, x3, preferred_element_type=F32)) + carry_ref[0:1, :]
    carry_ref[0:1, :] = f[tt - 1:tt, :]
    f1, f2, f3 = _split3(f)
    pieces = jnp.concatenate([f1, f2, f3], axis=1)
    kx_ref[0] = (ones_ref[...] - jnp.dot(pieces, sel_ref[...], preferred_element_type=F32)).astype(BF16)
    r = lax.broadcasted_iota(jnp.int32, (BF16_ROWS, LANES), 0)
    c = lax.broadcasted_iota(jnp.int32, (BF16_ROWS, LANES), 1)
    eye = jnp.where(r == c, 1.0, 0.0).astype(BF16)
    dn = (((1,), (1,)), ((), ()))
    ft = (lax.dot_general(eye, f1, dn, preferred_element_type=F32)
          + lax.dot_general(eye, f2, dn, preferred_element_type=F32)
          + lax.dot_general(eye, f3, dn, preferred_element_type=F32))
    ft_ref[0] = ft[0:C_HEADS, :]


def _fox_sel():
    sel = np.zeros((3 * LANES, C_HEADS // 2 * LANES), np.float32)
    ones = np.zeros((1, C_HEADS // 2 * LANES), np.float32)
    for pair in range(C_HEADS // 2):
        for half in (0, 1):
            for piece in range(3):
                sel[piece * LANES + 2 * pair + half, pair * LANES + 3 * half + piece] = 1.0
        ones[0, pair * LANES + EXT_ONES_K:pair * LANES + EXT_USED] = 1.0
    return jnp.asarray(sel, BF16), jnp.asarray(ones, F32)


def _fox_prep(logf, tt):
    b, t, _ = logf.shape
    sel, ones = _fox_sel()
    return pl.pallas_call(
        functools.partial(_fox_prep_kernel, tt=tt),
        grid=(b, t // tt),
        in_specs=[pl.BlockSpec((1, tt, LANES), lambda i, j: (i, j, 0)),
                  pl.BlockSpec(sel.shape, lambda i, j: (0, 0)),
                  pl.BlockSpec(ones.shape, lambda i, j: (0, 0))],
        out_specs=[pl.BlockSpec((1, tt, sel.shape[1]), lambda i, j: (i, j, 0)),
                   pl.BlockSpec((1, C_HEADS, tt), lambda i, j: (i, 0, j))],
        out_shape=[jax.ShapeDtypeStruct((b, t, sel.shape[1]), BF16),
                   jax.ShapeDtypeStruct((b, C_HEADS, t), F32)],
        scratch_shapes=[pltpu.VMEM((8, LANES), F32)],
        compiler_params=_params(2),
        name="fox_prep",
    )(logf, sel, ones)


def _flash_kernel(q_ref, k_ref, kx_ref, vt_ref, mb_ref, e1_ref, e2_ref, e3_ref, o_ref,
                  qt_ref, m_ref, acc_ref, mx0_ref, mx1_ref, *, mode, tq, tk, q_off, lam_init):
    h = pl.program_id(1)
    q_start = q_off + pl.program_id(2) * tq
    row_lo = lax.broadcasted_iota(jnp.int32, (LANES, 1), 0) < HEAD_DIM

    qt = q_ref[0].astype(F32).T
    zero = jnp.zeros_like(qt)
    r16 = lax.broadcasted_iota(jnp.int32, (BF16_ROWS, 1), 0)
    for half in (0, 1):
        qt_ref[half, 0:LANES, :] = (jnp.where(row_lo, qt, zero) if half == 0
                                    else jnp.where(row_lo, zero, qt)).astype(BF16)
        if mode == "diff":
            qpos = q_start + lax.broadcasted_iota(jnp.int32, (1, tq), 1)
            b = -e1_ref[h] * qpos.astype(F32)
        else:
            b = e1_ref[0, 0][half:half + 1, :]
        b1, b2, b3 = (z.astype(F32) for z in _split3(b))
        own = (r16 >= 3 * half) & (r16 < 3 * half + 3)
        ext = jnp.where(own, 1.0, jnp.where(r16 == EXT_ONES_K, b1, jnp.where(
            r16 == EXT_ONES_K + 1, b2, jnp.where(r16 == EXT_ONES_K + 2, b3, 0.0))))
        qt_ref[half, LANES:LANES + BF16_ROWS, :] = ext.astype(BF16)
        qt_ref[half, LANES + BF16_ROWS:, :] = jnp.zeros((LANES - BF16_ROWS, tq), BF16)
    m_ref[...] = jnp.full(m_ref.shape, NEG_INF, F32)
    acc_ref[...] = jnp.zeros(acc_ref.shape, F32)

    n_int = lax.div(q_start, tk)
    n_pairs = lax.div(n_int, 2)

    def key_tile(t):
        ks = pl.multiple_of(t * tk, tk)
        return jnp.concatenate([k_ref[0, pl.ds(ks, tk), :].astype(BF16), kx_ref[0, pl.ds(ks, tk), :]], axis=1)

    def colmax(t, mx_ref):
        kaug = key_tile(t)
        for half in (0, 1):
            mx_ref[half] = jnp.max(jnp.dot(kaug, qt_ref[half], preferred_element_type=F32), axis=0, keepdims=True)

    def soft_values(t, mx_ref, masked):
        kaug = key_tile(t)
        vt = jnp.concatenate([vt_ref[0, t], jnp.ones((BF16_ROWS, tk), BF16)], axis=0)
        for half in (0, 1):
            s = jnp.dot(kaug, qt_ref[half], preferred_element_type=F32)
            if masked:
                s = s + mb_ref[0]
            m_prev = m_ref[half]
            m_new = jnp.maximum(m_prev, jnp.max(s, axis=0, keepdims=True) if masked else mx_ref[half])
            pr = jnp.exp(s - m_new).astype(BF16)
            acc_ref[half] = jnp.exp(m_prev - m_new) * acc_ref[half] + jnp.dot(vt, pr, preferred_element_type=F32)
            m_ref[half] = m_new

    def pair_body(i, carry):
        t = 2 * i
        colmax(t + 1, mx1_ref)
        soft_values(t, mx0_ref, False)
        colmax(t + 2, mx0_ref)
        soft_values(t + 1, mx1_ref, False)
        return carry

    colmax(0, mx0_ref)
    lax.fori_loop(0, n_pairs, pair_body, 0)
    t0 = 2 * n_pairs

    @pl.when(n_int > t0)
    def _():
        soft_values(t0, mx0_ref, False)
        soft_values(t0 + 1, mx1_ref, True)

    @pl.when(n_int == t0)
    def _():
        soft_values(t0, mx0_ref, True)

    a0 = acc_ref[0]
    a1 = acc_ref[1]
    o0 = a0[0:LANES] * (1.0 / a0[LANES:LANES + 1])
    o1 = a1[0:LANES] * (1.0 / a1[LANES:LANES + 1])
    if mode == "diff":
        a = e2_ref[...]
        lam = (jnp.exp(jnp.sum(a[0:1] * a[1:2], axis=1, keepdims=True))
               - jnp.exp(jnp.sum(a[2:3] * a[3:4], axis=1, keepdims=True)) + lam_init)
        o = (o0 - lam * o1).T
        o_ref[0] = (_rms(o, e3_ref[...]) * (1.0 - lam_init)).astype(o_ref.dtype)
    else:
        o_ref[0] = jnp.where(row_lo, o0, o1).T.astype(o_ref.dtype)


def _flash(mode, q, k, kx, vt, e1, e2, e3, *, tq, tk, q_off, lam_init=0.0):
    b, t_q, _ = q.shape
    t_k = k.shape[1]
    n_col = q.shape[2] // LANES
    nt = t_k // tk
    assert tk % tq == 0 and q_off % tq == 0 and t_k >= q_off + t_q and (tq == tk or t_q == tq)
    kpos = jnp.arange(tk, dtype=jnp.int32)[:, None]
    qpos = q_off % tk + jnp.arange(tq, dtype=jnp.int32)[None, :]
    if mode == "diff":
        corr = -2.0 * e1[:, None, None] * jnp.maximum(kpos - qpos, 0).astype(F32)[None]
        mask_bias = jnp.where(((kpos >> 6) <= (qpos >> 6))[None], corr, NEG_INF)
        mb_spec = pl.BlockSpec((1, tk, tq), lambda bb, hh, qq: (hh, 0, 0))
    else:
        mask_bias = jnp.where(kpos <= qpos, 0.0, NEG_INF).astype(F32)[None]
        mb_spec = pl.BlockSpec((1, tk, tq), lambda bb, hh, qq: (0, 0, 0))
    const2 = lambda shape: pl.BlockSpec(shape, lambda bb, hh, qq: (0, 0))
    if mode == "diff":
        kx_spec = pl.BlockSpec((1, t_k, LANES), lambda bb, hh, qq: (hh, 0, 0))
        e1_spec = pl.BlockSpec(memory_space=pltpu.SMEM)
    else:
        q_blk0 = q_off // tq
        kx_spec = pl.BlockSpec((1, t_k, LANES), lambda bb, hh, qq: (bb, 0, hh))
        e1_spec = pl.BlockSpec((1, 1, 2, tq), lambda bb, hh, qq: (bb, hh, 0, q_blk0 + qq))
    kern = functools.partial(_flash_kernel, mode=mode, tq=tq, tk=tk, q_off=q_off, lam_init=lam_init)
    return pl.pallas_call(
        kern,
        grid=(b, n_col, t_q // tq),
        in_specs=[pl.BlockSpec((1, tq, LANES), lambda bb, hh, qq: (bb, qq, hh)),
                  pl.BlockSpec((1, t_k, LANES), lambda bb, hh, qq: (bb, 0, hh)),
                  kx_spec,
                  pl.BlockSpec((1, nt, LANES, tk), lambda bb, hh, qq: (bb, 0, hh, 0)),
                  mb_spec, e1_spec, const2(e2.shape), const2(e3.shape)],
        out_specs=pl.BlockSpec((1, tq, LANES), lambda bb, hh, qq: (bb, qq, hh)),
        out_shape=jax.ShapeDtypeStruct(q.shape, BF16),
        scratch_shapes=[pltpu.VMEM((2, 2 * LANES, tq), BF16),
                        pltpu.VMEM((2, 1, tq), F32),
                        pltpu.VMEM((2, LANES + BF16_ROWS, tq), F32),
                        pltpu.VMEM((2, 1, tq), F32), pltpu.VMEM((2, 1, tq), F32)],
        compiler_params=_params(3),
        name="flash_" + mode,
    )(q, k, kx, vt, mask_bias, e1, e2, e3)


def _alibi_key_cols(slopes, t_k):
    pos = jnp.arange(t_k, dtype=jnp.int32)
    a1 = slopes[:, None] * ((pos >> 7) << 7).astype(F32)[None, :]
    a2 = slopes[:, None] * (pos & 127).astype(F32)[None, :]
    zero = jnp.zeros_like(a1)
    one = jnp.ones_like(a1)
    cols = jnp.stack([a1, a2, zero, a1, a2, zero, one, one, one], axis=-1)
    return jnp.pad(cols, ((0, 0), (0, 0), (0, LANES - EXT_USED))).astype(BF16)


def _band_kernel(q_ref, *rest, nblk, rows, thr0):
    k_refs = rest[:nblk]
    v_refs = rest[nblk:2 * nblk]
    bias_ref = rest[2 * nblk]
    o_ref = rest[2 * nblk + 1]
    c = pl.program_id(1)
    lk = nblk * rows
    valid = lax.broadcasted_iota(jnp.int32, (1, lk), 1) >= (thr0 - c * rows)
    lo = lax.broadcasted_iota(jnp.int32, (1, LANES), 1) < HEAD_DIM
    for pair in range(B_HEADS // 2):
        sl = slice(LANES * pair, LANES * pair + LANES)
        q = q_ref[0, :, sl]
        zero = jnp.zeros_like(q)
        k = jnp.concatenate([r[0, :, sl] for r in k_refs], axis=0).astype(BF16)
        v = jnp.concatenate([r[0, :, sl] for r in v_refs], axis=0).astype(BF16)
        outs = []
        for half in (0, 1):
            qh = jnp.where(lo, q, zero) if half == 0 else jnp.where(lo, zero, q)
            s = lax.dot_general(qh, k, (((1,), (1,)), ((), ())), preferred_element_type=F32)
            s = jnp.where(valid, s + bias_ref[2 * pair + half], NEG_INF)
            m = jnp.max(s, axis=1, keepdims=True)
            pr = jnp.exp(s - m)
            den = jnp.sum(pr, axis=1, keepdims=True)
            outs.append(jnp.dot(pr.astype(BF16), v, preferred_element_type=F32) * (1.0 / den))
        o_ref[0, :, sl] = jnp.where(lo, outs[0], outs[1]).astype(o_ref.dtype)


def _band_keys(rows):
    return -(-(B_WIN + rows) // LANES) * LANES


def _band_bias(table, rows):
    lk = _band_keys(rows)
    w = rows + lk - 1
    d = np.concatenate([np.arange(lk), np.arange(-(rows - 1), 0)])
    idx = np.clip(B_WIN - d, -B_REL_CLIP, B_REL_CLIP) + B_REL_CLIP
    e = table.astype(F32)[:, idx]
    toep = jnp.tile(e, (1, rows))[:, :rows * (w - 1)].reshape(-1, rows, w - 1)[:, :, :lk]
    i = np.arange(rows)[:, None]
    j = np.arange(lk)[None, :]
    lo_edge = (i // CHUNK) * CHUNK
    inband = (j >= lo_edge) & (j < lo_edge + B_WIN + CHUNK)
    return jnp.where(jnp.asarray(inband)[None], toep, NEG_INF)


def _band(q, k, v, table, *, rows, off):
    b, t, w = q.shape
    nblk = _band_keys(rows) // rows
    shift = (B_WIN - off) // rows
    last = k.shape[1] // rows - 1
    bias = _band_bias(table, rows)
    kv_specs = [pl.BlockSpec((1, rows, w), lambda bb, cc, i=i: (bb, jnp.clip(cc + i - shift, 0, last), 0))
                for i in range(nblk)]
    kern = functools.partial(_band_kernel, nblk=nblk, rows=rows, thr0=B_WIN - off)
    return pl.pallas_call(
        kern,
        grid=(b, t // rows),
        in_specs=[pl.BlockSpec((1, rows, w), lambda bb, cc: (bb, cc, 0))] + kv_specs + kv_specs
                 + [pl.BlockSpec(bias.shape, lambda bb, cc: (0, 0, 0))],
        out_specs=pl.BlockSpec((1, rows, w), lambda bb, cc: (bb, cc, 0)),
        out_shape=jax.ShapeDtypeStruct(q.shape, BF16),
        compiler_params=_params(2),
        name="band_attn",
    )(q, *([k] * nblk), *([v] * nblk), bias)


def _gelu_tanh(x):
    return 0.5 * x * (1.0 + jnp.tanh(math.sqrt(2.0 / math.pi) * (x + 0.044715 * (x * x * x))))


def _rglru_kernel(dx_ref, dg_ref, buf_ref, h0_ref, cw_ref, cb_ref, wa_ref, ba_ref, wx_ref, bx_ref, lam_ref,
                  od_ref, nbuf_ref, hl_ref, tail_ref, hst_ref, a_s, b_s, h_s, *, tt):
    j = pl.program_id(1)
    nj = pl.num_programs(1)

    @pl.when(j == 0)
    def _():
        tail_ref[...] = jnp.zeros(tail_ref.shape, F32)
        tail_ref[5:8, :] = buf_ref[0]
        hst_ref[...] = h0_ref[0]

    x = dx_ref[0]
    xs = jnp.concatenate([tail_ref[...], x], axis=0)
    cw = cw_ref[...]
    u = cb_ref[...] + cw[3:4] * x
    for tap in range(D_CONV - 1):
        u = u + cw[tap:tap + 1] * xs[5 + tap:5 + tap + tt]
    tail_ref[...] = x[tt - 8:tt]

    ub = u.astype(BF16)
    r = _sigmoid(jnp.dot(ub, wa_ref[...], preferred_element_type=F32) + ba_ref[...])
    gate_i = _sigmoid(jnp.dot(ub, wx_ref[...], preferred_element_type=F32) + bx_ref[...])
    log_a = -RG_C * r * _softplus(-lam_ref[...])
    a_s[...] = jnp.exp(log_a)
    b_s[...] = jnp.sqrt(-_expm1(2.0 * log_a)) * (gate_i * u)

    def body(i, h):
        base = pl.multiple_of(i * 8, 8)
        for rr in range(8):
            h = a_s[pl.ds(base + rr, 1), :] * h + b_s[pl.ds(base + rr, 1), :]
            h_s[pl.ds(base + rr, 1), :] = h
        return h

    h_fin = lax.fori_loop(0, tt // 8, body, hst_ref[...])
    hst_ref[...] = h_fin
    od_ref[0] = (h_s[...] * _gelu_tanh(dg_ref[0])).astype(od_ref.dtype)

    @pl.when(j == nj - 1)
    def _():
        nbuf_ref[0] = tail_ref[8 - (D_CONV - 1):8, :]
        hl_ref[0] = h_fin


def _rglru(dx, dg, buf, h0, cw, cb, wa, ba, wx, bx, lam, *, tt):
    b, t, w = dx.shape
    const2 = lambda shape: pl.BlockSpec(shape, lambda bb, jj: (0, 0))
    return pl.pallas_call(
        functools.partial(_rglru_kernel, tt=tt),
        grid=(b, t // tt),
        in_specs=[pl.BlockSpec((1, tt, w), lambda bb, jj: (bb, jj, 0)),
                  pl.BlockSpec((1, tt, w), lambda bb, jj: (bb, jj, 0)),
                  pl.BlockSpec((1, D_CONV - 1, w), lambda bb, jj: (bb, 0, 0)),
                  pl.BlockSpec((1, 1, w), lambda bb, jj: (bb, 0, 0)),
                  const2((D_CONV, w)), const2((1, w)), const2((w, w)), const2((1, w)),
                  const2((w, w)), const2((1, w)), const2((1, w))],
        out_specs=[pl.BlockSpec((1, tt, w), lambda bb, jj: (bb, jj, 0)),
                   pl.BlockSpec((1, D_CONV - 1, w), lambda bb, jj: (bb, 0, 0)),
                   pl.BlockSpec((1, 1, w), lambda bb, jj: (bb, 0, 0))],
        out_shape=[jax.ShapeDtypeStruct((b, t, w), BF16),
                   jax.ShapeDtypeStruct((b, D_CONV - 1, w), F32),
                   jax.ShapeDtypeStruct((b, 1, w), F32)],
        scratch_shapes=[pltpu.VMEM((8, w), F32), pltpu.VMEM((1, w), F32),
                        pltpu.VMEM((tt, w), F32), pltpu.VMEM((tt, w), F32), pltpu.VMEM((tt, w), F32)],
        compiler_params=_params(2),
        name="rglru",
    )(dx, dg, buf, h0, cw, cb, wa, ba, wx, bx, lam)


def _tail_kernel(x_ref, o1_ref, o2_ref, wo_ref, g_ref, w1_ref, w3_ref, w2_ref, fg_ref, out_ref,
                 y_ref, h_ref, acc_ref, *, final_norm):
    j = pl.program_id(1)
    half = wo_ref.shape[0] // 2

    @pl.when(j == 0)
    def _():
        y = (x_ref[...]
             + jnp.dot(o1_ref[...], wo_ref[0:half, :], preferred_element_type=F32)
             + jnp.dot(o2_ref[...], wo_ref[half:, :], preferred_element_type=F32))
        y_ref[...] = y
        h_ref[...] = _rms(y, g_ref[...]).astype(BF16)
        acc_ref[...] = jnp.zeros(acc_ref.shape, F32)

    h = h_ref[...]
    a = jnp.dot(h, w1_ref[...], preferred_element_type=F32)
    g = jnp.dot(h, w3_ref[...], preferred_element_type=F32)
    act = (a * _sigmoid(a) * g).astype(BF16)
    acc_ref[...] += jnp.dot(act, w2_ref[...], preferred_element_type=F32)

    @pl.when(j == pl.num_programs(1) - 1)
    def _():
        y = y_ref[...] + acc_ref[...]
        if final_norm:
            y = _rms(y, fg_ref[...])
        out_ref[...] = y


def _tail(x, o1, o2, wo, g, w1, w3, w2, fg, *, final_norm, tm, hc):
    n = x.shape[0]
    hw = o1.shape[1]
    return pl.pallas_call(
        functools.partial(_tail_kernel, final_norm=final_norm),
        grid=(n // tm, FFN_HIDDEN // hc),
        in_specs=[pl.BlockSpec((tm, D_MODEL), lambda i, j: (i, 0)),
                  pl.BlockSpec((tm, hw), lambda i, j: (i, 0)),
                  pl.BlockSpec((tm, hw), lambda i, j: (i, 0)),
                  pl.BlockSpec(wo.shape, lambda i, j: (0, 0)),
                  pl.BlockSpec((1, D_MODEL), lambda i, j: (0, 0)),
                  pl.BlockSpec((D_MODEL, hc), lambda i, j: (0, j)),
                  pl.BlockSpec((D_MODEL, hc), lambda i, j: (0, j)),
                  pl.BlockSpec((hc, D_MODEL), lambda i, j: (j, 0)),
                  pl.BlockSpec((1, D_MODEL), lambda i, j: (0, 0))],
        out_specs=pl.BlockSpec((tm, D_MODEL), lambda i, j: (i, 0)),
        out_shape=jax.ShapeDtypeStruct((n, D_MODEL), F32),
        scratch_shapes=[pltpu.VMEM((tm, D_MODEL), F32), pltpu.VMEM((tm, D_MODEL), BF16),
                        pltpu.VMEM((tm, D_MODEL), F32)],
        compiler_params=_params(2, 56),
        name="tail_ffn",
    )(x, o1, o2, wo, g, w1, w3, w2, fg)


def _pad_time(x, front, back):
    return jnp.pad(x, ((0, 0), (front, back), (0, 0)))


def _block_diag(w):
    n, d, _ = w.shape
    out = jnp.zeros((n * d, n * d), w.dtype)
    for i in range(n):
        out = out.at[i * d:(i + 1) * d, i * d:(i + 1) * d].set(w[i])
    return out


def _pad_queries(q, tq):
    return _pad_time(q, 0, (-q.shape[1]) % tq)


def _vt_tiles(v, tile):
    b, t_k, w = v.shape
    return jnp.swapaxes(v.reshape(b, t_k // tile, tile, w), 2, 3).astype(BF16)


def _layer_ab(x, past, wts, *, tm, tq, tk, rows):
    b, t, _ = x.shape
    n = b * t
    w_in, slopes, a_lambda, a_subln_g, b_rel_bias, g_mix = wts
    prompt = past is None
    assert not prompt or tm == tk
    aq, ak, av, bq, bk, bv, *vts = _norm_proj(
        x.reshape(n, D_MODEL), g_mix, w_in, jnp.zeros((1, LANES), F32),
        [512] * 6, [BF16, F32, F32, BF16, F32, F32], [SCALE, 1.0, 1.0, SCALE, 1.0, 1.0], False, tm,
        t_segs=(2,) if prompt else ())
    r3 = lambda z: z.reshape(b, t, 512)
    aq, ak, av, bq, bk, bv = map(r3, (aq, ak, av, bq, bk, bv))
    if prompt:
        ka, q_off = ak, 0
        vta = vts[0].reshape(b, t // tm, 512, tm)
        kb, vb, off = bk, bv, 0
        keep = min(B_WIN, t)
        nbk, nbv = bk[:, t - keep:], bv[:, t - keep:]
    else:
        cak, cav, cbk, cbv = past
        p_len = cak.shape[1]
        back = (-(p_len + t)) % tk
        ka = _pad_time(jnp.concatenate([cak.reshape(b, p_len, 512), ak], axis=1), 0, back)
        vta = _vt_tiles(_pad_time(jnp.concatenate([cav.reshape(b, p_len, 512), av], axis=1), 0, back), tk)
        q_off = p_len
        lb = cbk.shape[1]
        kb = jnp.concatenate([cbk.reshape(b, lb, 512), bk], axis=1)
        vb = jnp.concatenate([cbv.reshape(b, lb, 512), bv], axis=1)
        off = lb
        nbk, nbv = kb[:, t:], vb[:, t:]
    o_a = _flash("diff", _pad_queries(aq, tq), ka, _alibi_key_cols(slopes, ka.shape[1]), vta,
                 slopes, a_lambda, a_subln_g.reshape(1, LANES),
                 tq=tq, tk=tk, q_off=q_off, lam_init=_lambda_init(0))[:, :t]
    o_b = _band(bq, kb, vb, b_rel_bias, rows=rows, off=off)
    state = (ak.reshape(b, t, A_HEADS, 2 * HEAD_DIM), av.reshape(b, t, A_HEADS, 2 * HEAD_DIM),
             nbk.reshape(b, -1, B_HEADS, HEAD_DIM), nbv.reshape(b, -1, B_HEADS, HEAD_DIM))
    return o_a.reshape(n, 512), o_b.reshape(n, 512), state


def _layer_cd(x, past, wts, *, tm, tq, tk, tt, tc):
    b, t, _ = x.shape
    n = b * t
    w_in, fb, cw, cb, wa, ba, wx, bx, lam, g_mix = wts
    prompt = past is None
    assert not prompt or tm == tk
    cq, ck, cv, dx, dg, logf, *vts = _norm_proj(
        x.reshape(n, D_MODEL), g_mix, w_in, fb,
        [512] * 5 + [LANES], [BF16, F32, F32, F32, F32, F32], [SCALE, 1.0, 1.0, 1.0, 1.0, 1.0], True, tm,
        t_segs=(2,) if prompt else ())
    r3 = lambda z: z.reshape(b, t, z.shape[-1])
    cq, ck, cv, dx, dg, logf = map(r3, (cq, ck, cv, dx, dg, logf))
    if prompt:
        kc, lf_all, q_off = ck, logf, 0
        vtc = vts[0].reshape(b, t // tm, 512, tm)
        buf = jnp.zeros((b, D_CONV - 1, D_WIDTH), F32)
        h0 = jnp.zeros((b, 1, D_WIDTH), F32)
    else:
        cck, ccv, cclogf, buf, h0 = past
        p_len = cck.shape[1]
        back = (-(p_len + t)) % tk
        kc = _pad_time(jnp.concatenate([cck.reshape(b, p_len, 512), ck], axis=1), 0, back)
        vtc = _vt_tiles(_pad_time(jnp.concatenate([ccv.reshape(b, p_len, 512), cv], axis=1), 0, back), tk)
        q_off = p_len
        lf_all = _pad_time(jnp.concatenate(
            [jnp.pad(cclogf.astype(F32), ((0, 0), (0, 0), (0, LANES - C_HEADS))), logf], axis=1), 0, back)
        h0 = h0.reshape(b, 1, D_WIDTH)
    kx, f_rows = _fox_prep(lf_all, tc)
    f_rows = f_rows.reshape(b, C_HEADS // 2, 2, -1)
    dummy = jnp.zeros((1, LANES), F32)
    o_c = _flash("fox", _pad_queries(cq, tq), kc, kx, vtc, f_rows, dummy, dummy,
                 tq=tq, tk=tk, q_off=q_off)[:, :t]
    o_d, nbuf, hl = _rglru(dx, dg, buf, h0, cw, cb, wa, ba, wx, bx, lam, tt=tt)
    state = (ck.reshape(b, t, C_HEADS, HEAD_DIM), cv.reshape(b, t, C_HEADS, HEAD_DIM),
             logf[:, :, :C_HEADS], nbuf, hl.reshape(b, D_WIDTH))
    return o_c.reshape(n, 512), o_d.reshape(n, 512), state


def kernel(x_prompt, x_sample, cache_a_k, cache_a_v, cache_b_k, cache_b_v, cache_c_k, cache_c_v, cache_c_logf, state_d_conv, state_d_h, norm_mix_g, norm_ffn_g, ab_w_in, ab_w_out, a_lambda, a_subln_g, b_rel_bias, cd_w_in, cd_w_out, c_f_bias, d_conv_w, d_conv_b, d_w_a, d_b_a, d_w_x, d_b_x, d_lambda, ffn_w1, ffn_w3, ffn_w2, final_g):
    slopes = jnp.asarray([2.0 ** (-8.0 * (h + 1) / A_HEADS) for h in range(A_HEADS)], F32)
    row = lambda z: z.reshape(1, -1).astype(F32)

    ab_wts = (ab_w_in.astype(BF16), slopes, a_lambda.astype(F32), a_subln_g.astype(F32), b_rel_bias,
              row(norm_mix_g[0]))
    w_cd = jnp.concatenate([cd_w_in[:, :1536], cd_w_in[:, 1544:], cd_w_in[:, 1536:1544],
                            jnp.zeros((D_MODEL, LANES - C_HEADS), cd_w_in.dtype)], axis=1).astype(BF16)
    fb = jnp.pad(c_f_bias.astype(F32), (0, LANES - C_HEADS)).reshape(1, LANES)
    cd_wts = (w_cd, fb, d_conv_w.astype(F32), row(d_conv_b), _block_diag(d_w_a).astype(BF16), row(d_b_a),
              _block_diag(d_w_x).astype(BF16), row(d_b_x), row(d_lambda), row(norm_mix_g[1]))
    wo = (ab_w_out.astype(BF16), cd_w_out.astype(BF16))
    w1, w3, w2 = ffn_w1.astype(BF16), ffn_w3.astype(BF16), ffn_w2.astype(BF16)
    fg = row(final_g)

    def trunk(x, past_ab, past_cd, cfg):
        b, t, _ = x.shape
        o_a, o_b, st_ab = _layer_ab(x, past_ab, ab_wts, tm=cfg["tm"], tq=cfg["tq"], tk=cfg["tk"],
                                    rows=cfg["rows"])
        y = _tail(x.reshape(b * t, D_MODEL), o_a, o_b, wo[0], row(norm_ffn_g[0]), w1[0], w3[0], w2[0], fg,
                  final_norm=False, tm=cfg["tm"], hc=cfg["hc"])
        o_c, o_d, st_cd = _layer_cd(y.reshape(b, t, D_MODEL), past_cd, cd_wts, tm=cfg["tm"], tq=cfg["tq"],
                                    tk=cfg["tk"], tt=cfg["tt"], tc=cfg["tc"])
        y = _tail(y, o_c, o_d, wo[1], row(norm_ffn_g[1]), w1[1], w3[1], w2[1], fg,
                  final_norm=True, tm=cfg["tm"], hc=cfg["hc"])
        return (y.reshape(b, t, D_MODEL),) + st_ab + st_cd

    prompt_cfg = dict(tm=512, tq=512, tk=512, rows=256, hc=1408, tt=512, tc=512)
    sample_cfg = dict(tm=512, tq=128, tk=512, rows=64, hc=1408, tt=64, tc=512)
    outp = trunk(x_prompt, None, None, prompt_cfg)
    outs = trunk(x_sample, (cache_a_k, cache_a_v, cache_b_k, cache_b_v),
                 (cache_c_k, cache_c_v, cache_c_logf, state_d_conv, state_d_h), sample_cfg)
    return (outp[0], outs[0]) + outp[1:] + outs[1:]
```

```python
import functools
import math

import numpy as np
import jax
import jax.numpy as jnp
from jax import lax
from jax.experimental import pallas as pl
from jax.experimental.pallas import tpu as pltpu

F32 = jnp.float32
BF16 = jnp.bfloat16

D_MODEL = 1024
CHUNK = 64
HEAD_DIM = 64
EPS = 1e-6
NEG_INF = -1e30
SCALE = HEAD_DIM ** -0.5
A_HEADS = 4
B_HEADS = 8
B_WIN = 512
B_REL_CLIP = 128
C_HEADS = 8
D_WIDTH = 512
D_CONV = 4
RG_C = 8.0
FFN_HIDDEN = 2816
LANES = 128
BF16_ROWS = 16
MIB = 1024 * 1024

EXT_ONES_K = 6
EXT_USED = 9


def _lambda_init(layer):
    return 0.8 - 0.6 * math.exp(-0.3 * layer)


def _params(n_axes, vmem_mib=48):
    return pltpu.CompilerParams(dimension_semantics=("arbitrary",) * n_axes,
                                vmem_limit_bytes=vmem_mib * MIB)


def _rms(x, g):
    return x * lax.rsqrt(jnp.mean(x * x, axis=-1, keepdims=True) + EPS) * g


def _sigmoid(x):
    return 1.0 / (1.0 + jnp.exp(-x))


def _expm1(x):
    u = jnp.exp(x)
    return jnp.where(u == 1.0, x, (u - 1.0) * x / jnp.log(jnp.where(u == 1.0, 2.0, u)))


def _softplus(x):
    return jnp.maximum(x, 0.0) + jnp.log1p(jnp.exp(-jnp.abs(x)))


def _split3(x):
    x1 = x.astype(BF16)
    r1 = x - x1.astype(F32)
    x2 = r1.astype(BF16)
    x3 = (r1 - x2.astype(F32)).astype(BF16)
    return x1, x2, x3


def _norm_proj_kernel(x_ref, g_ref, w_ref, fb_ref, *out_refs, widths, scales, logsig_last, extras):
    h = _rms(x_ref[...], g_ref[...]).astype(BF16)
    extra_refs = out_refs[len(widths):]
    off = 0
    for idx, (o_ref, wd) in enumerate(zip(out_refs, widths)):
        y = jnp.dot(h, w_ref[:, off:off + wd], preferred_element_type=F32)
        if logsig_last and idx == len(widths) - 1:
            y = -_softplus(-(y + fb_ref[...]))
        if scales[idx] != 1.0:
            y = y * scales[idx]
        o_ref[...] = y.astype(o_ref.dtype)
        kinds = {kind: r for (seg, kind), r in zip(extras, extra_refs) if seg == idx}
        if "tile_t" in kinds or "chan" in kinds:
            yt = y.T
            if "tile_t" in kinds:
                kinds["tile_t"][0] = yt.astype(BF16)
            if "chan" in kinds:
                kinds["chan"][0] = yt
        if "heads" in kinds:
            for head in range(wd // LANES):
                kinds["heads"][:, head, :] = y[:, LANES * head:LANES * head + LANES]
        off += wd


def _norm_proj(x, g, w, fb, widths, dtypes, scales, logsig_last, tm, extras=(), seq=None):
    n = x.shape[0]
    kern = functools.partial(_norm_proj_kernel, widths=tuple(widths), scales=tuple(scales),
                             logsig_last=logsig_last, extras=tuple(extras))
    extra_specs, extra_shapes = [], []
    for seg, kind in extras:
        wd = widths[seg]
        if kind == "tile_t":
            extra_specs.append(pl.BlockSpec((1, wd, tm), lambda i: (i, 0, 0)))
            extra_shapes.append(jax.ShapeDtypeStruct((n // tm, wd, tm), BF16))
        elif kind == "heads":
            extra_specs.append(pl.BlockSpec((tm, wd // LANES, LANES), lambda i: (i, 0, 0)))
            extra_shapes.append(jax.ShapeDtypeStruct((n, wd // LANES, LANES), F32))
        else:
            per = seq // tm
            extra_specs.append(pl.BlockSpec((1, wd, tm), lambda i, per=per: (i // per, 0, i % per)))
            extra_shapes.append(jax.ShapeDtypeStruct((n // seq, wd, seq), F32))
    return pl.pallas_call(
        kern,
        grid=(n // tm,),
        in_specs=[pl.BlockSpec((tm, D_MODEL), lambda i: (i, 0)),
                  pl.BlockSpec((1, D_MODEL), lambda i: (0, 0)),
                  pl.BlockSpec(w.shape, lambda i: (0, 0)),
                  pl.BlockSpec((1, LANES), lambda i: (0, 0))],
        out_specs=[pl.BlockSpec((tm, wd), lambda i: (i, 0)) for wd in widths] + extra_specs,
        out_shape=[jax.ShapeDtypeStruct((n, wd), dt) for wd, dt in zip(widths, dtypes)] + extra_shapes,
        compiler_params=_params(1),
        name="norm_proj",
    )(x, g, w, fb)


def _fox_prep_kernel(x_ref, sel_ref, ones_ref, kx_ref, ft_ref, carry_ref, *, tt):
    @pl.when(pl.program_id(1) == 0)
    def _():
        carry_ref[...] = jnp.zeros(carry_ref.shape, F32)

    lane = lax.broadcasted_iota(jnp.int32, (1, LANES), 1)
    x = jnp.where(lane < C_HEADS, x_ref[0], 0.0)
    row = lax.broadcasted_iota(jnp.int32, (tt, tt), 0)
    col = lax.broadcasted_iota(jnp.int32, (tt, tt), 1)
    tri = jnp.where(col <= row, 1.0, 0.0).astype(BF16)
    x1, x2, x3 = _split3(x)
    f = (jnp.dot(tri, x1, preferred_element_type=F32) + jnp.dot(tri, x2, preferred_element_type=F32)
         + jnp.dot(tri, x3, preferred_element_type=F32)) + carry_ref[0:1, :]
    carry_ref[0:1, :] = f[tt - 1:tt, :]
    f1, f2, f3 = _split3(f)
    pieces = jnp.concatenate([f1, f2, f3], axis=1)
    kx_ref[0] = (ones_ref[...] - jnp.dot(pieces, sel_ref[...], preferred_element_type=F32)).astype(BF16)
    r = lax.broadcasted_iota(jnp.int32, (BF16_ROWS, LANES), 0)
    c = lax.broadcasted_iota(jnp.int32, (BF16_ROWS, LANES), 1)
    eye = jnp.where(r == c, 1.0, 0.0).astype(BF16)
    dn = (((1,), (1,)), ((), ()))
    ft = (lax.dot_general(eye, f1, dn, preferred_element_type=F32)
          + lax.dot_general(eye, f2, dn, preferred_element_type=F32)
          + lax.dot_general(eye, f3, dn, preferred_element_type=F32))
    ft_ref[0] = ft[0:C_HEADS, :]


def _fox_sel():
    sel = np.zeros((3 * LANES, C_HEADS // 2 * LANES), np.float32)
    ones = np.zeros((1, C_HEADS // 2 * LANES), np.float32)
    for pair in range(C_HEADS // 2):
        for half in (0, 1):
            for piece in range(3):
                sel[piece * LANES + 2 * pair + half, pair * LANES + 3 * half + piece] = 1.0
        ones[0, pair * LANES + EXT_ONES_K:pair * LANES + EXT_USED] = 1.0
    return jnp.asarray(sel, BF16), jnp.asarray(ones, F32)


def _fox_prep(logf, tt):
    b, t, _ = logf.shape
    sel, ones = _fox_sel()
    return pl.pallas_call(
        functools.partial(_fox_prep_kernel, tt=tt),
        grid=(b, t // tt),
        in_specs=[pl.BlockSpec((1, tt, LANES), lambda i, j: (i, j, 0)),
                  pl.BlockSpec(sel.shape, lambda i, j: (0, 0)),
                  pl.BlockSpec(ones.shape, lambda i, j: (0, 0))],
        out_specs=[pl.BlockSpec((1, tt, sel.shape[1]), lambda i, j: (i, j, 0)),
                   pl.BlockSpec((1, C_HEADS, tt), lambda i, j: (i, 0, j))],
        out_shape=[jax.ShapeDtypeStruct((b, t, sel.shape[1]), BF16),
                   jax.ShapeDtypeStruct((b, C_HEADS, t), F32)],
        scratch_shapes=[pltpu.VMEM((8, LANES), F32)],
        compiler_params=_params(2),
        name="fox_prep",
    )(logf, sel, ones)


def _flash_kernel(q_ref, k_ref, kx_ref, vt_ref, mb_ref, e1_ref, e2_ref, e3_ref, o_ref,
                  qt_ref, m_ref, acc_ref, s0_ref, s1_ref, p0_ref, p1_ref, al0_ref, al1_ref, mx0_ref, mx1_ref,
                  *, mode, tq, tk, q_off, lam_init):
    h = pl.program_id(1)
    q_start = q_off + pl.program_id(2) * tq
    row_lo = lax.broadcasted_iota(jnp.int32, (LANES, 1), 0) < HEAD_DIM

    qt = q_ref[0].astype(F32).T
    zero = jnp.zeros_like(qt)
    r16 = lax.broadcasted_iota(jnp.int32, (BF16_ROWS, 1), 0)
    for half in (0, 1):
        qt_ref[half, 0:LANES, :] = (jnp.where(row_lo, qt, zero) if half == 0
                                    else jnp.where(row_lo, zero, qt)).astype(BF16)
        if mode == "diff":
            qpos = q_start + lax.broadcasted_iota(jnp.int32, (1, tq), 1)
            b = -e1_ref[h] * qpos.astype(F32)
        else:
            b = e1_ref[0, 0][half:half + 1, :]
        b1, b2, b3 = (z.astype(F32) for z in _split3(b))
        own = (r16 >= 3 * half) & (r16 < 3 * half + 3)
        ext = jnp.where(own, 1.0, jnp.where(r16 == EXT_ONES_K, b1, jnp.where(
            r16 == EXT_ONES_K + 1, b2, jnp.where(r16 == EXT_ONES_K + 2, b3, 0.0))))
        qt_ref[half, LANES:LANES + BF16_ROWS, :] = ext.astype(BF16)
        qt_ref[half, LANES + BF16_ROWS:, :] = jnp.zeros((LANES - BF16_ROWS, tq), BF16)
    m_ref[...] = jnp.full(m_ref.shape, NEG_INF, F32)
    acc_ref[...] = jnp.zeros(acc_ref.shape, F32)
    p1_ref[...] = jnp.zeros(p1_ref.shape, BF16)
    al1_ref[...] = jnp.ones(al1_ref.shape, F32)

    def scores(t, buf):
        s_ref, mx_ref = buf[0], buf[1]
        ks = pl.multiple_of(t * tk, tk)
        kaug = jnp.concatenate([k_ref[0, pl.ds(ks, tk), :].astype(BF16), kx_ref[0, pl.ds(ks, tk), :]], axis=1)
        for half in (0, 1):
            s = jnp.dot(kaug, qt_ref[half], preferred_element_type=F32)
            s_ref[half] = s
            mx_ref[half] = jnp.max(s, axis=0, keepdims=True)

    def softmax(t, buf, masked):
        s_ref, mx_ref, p_ref, al_ref = buf
        for half in (0, 1):
            s = s_ref[half]
            if masked:
                s = s + mb_ref[0]
            m_prev = m_ref[half]
            m_new = jnp.maximum(m_prev, jnp.max(s, axis=0, keepdims=True) if masked else mx_ref[half])
            al_ref[half] = jnp.exp(m_prev - m_new)
            p_ref[half] = jnp.exp(s - m_new).astype(BF16)
            m_ref[half] = m_new

    def values(t, buf):
        p_ref, al_ref = buf[2], buf[3]
        vt = jnp.concatenate([vt_ref[0, jnp.maximum(t, 0)], jnp.ones((BF16_ROWS, tk), BF16)], axis=0)
        for half in (0, 1):
            acc_ref[half] = al_ref[half] * acc_ref[half] + jnp.dot(vt, p_ref[half], preferred_element_type=F32)

    n_int = lax.div(q_start, tk)
    n_pairs = lax.div(n_int, 2)

    even = (s0_ref, mx0_ref, p0_ref, al0_ref)
    odd = (s1_ref, mx1_ref, p1_ref, al1_ref)

    def pair_body(i, carry):
        t = 2 * i
        values(t - 1, odd)
        scores(t + 1, odd)
        softmax(t, even, False)
        values(t, even)
        scores(t + 2, even)
        softmax(t + 1, odd, False)
        return carry

    scores(0, even)
    lax.fori_loop(0, n_pairs, pair_body, 0)
    t0 = 2 * n_pairs

    @pl.when(n_int > t0)
    def _():
        values(t0 - 1, odd)
        scores(t0 + 1, odd)
        softmax(t0, even, False)
        values(t0, even)
        softmax(t0 + 1, odd, True)
        values(t0 + 1, odd)

    @pl.when(n_int == t0)
    def _():
        values(t0 - 1, odd)
        softmax(t0, even, True)
        values(t0, even)

    a0 = acc_ref[0]
    a1 = acc_ref[1]
    o0 = a0[0:LANES] * (1.0 / a0[LANES:LANES + 1])
    o1 = a1[0:LANES] * (1.0 / a1[LANES:LANES + 1])
    if mode == "diff":
        a = e2_ref[...]
        lam = (jnp.exp(jnp.sum(a[0:1] * a[1:2], axis=1, keepdims=True))
               - jnp.exp(jnp.sum(a[2:3] * a[3:4], axis=1, keepdims=True)) + lam_init)
        o = (o0 - lam * o1).T
        o_ref[0] = (_rms(o, e3_ref[...]) * (1.0 - lam_init)).astype(o_ref.dtype)
    else:
        o_ref[0] = jnp.where(row_lo, o0, o1).T.astype(o_ref.dtype)


def _flash(mode, q, k, kx, vt, e1, e2, e3, *, tq, tk, q_off, lam_init=0.0):
    b, t_q, _ = q.shape
    t_k = k.shape[1]
    n_col = q.shape[2] // LANES
    nt = t_k // tk
    assert tk % tq == 0 and q_off % tq == 0 and t_k >= q_off + t_q and (tq == tk or t_q == tq)
    kpos = jnp.arange(tk, dtype=jnp.int32)[:, None]
    qpos = q_off % tk + jnp.arange(tq, dtype=jnp.int32)[None, :]
    if mode == "diff":
        corr = -2.0 * e1[:, None, None] * jnp.maximum(kpos - qpos, 0).astype(F32)[None]
        mask_bias = jnp.where(((kpos >> 6) <= (qpos >> 6))[None], corr, NEG_INF)
        mb_spec = pl.BlockSpec((1, tk, tq), lambda bb, hh, qq: (hh, 0, 0))
    else:
        mask_bias = jnp.where(kpos <= qpos, 0.0, NEG_INF).astype(F32)[None]
        mb_spec = pl.BlockSpec((1, tk, tq), lambda bb, hh, qq: (0, 0, 0))
    const2 = lambda shape: pl.BlockSpec(shape, lambda bb, hh, qq: (0, 0))
    if mode == "diff":
        kx_spec = pl.BlockSpec((1, t_k, LANES), lambda bb, hh, qq: (hh, 0, 0))
        e1_spec = pl.BlockSpec(memory_space=pltpu.SMEM)
    else:
        q_blk0 = q_off // tq
        kx_spec = pl.BlockSpec((1, t_k, LANES), lambda bb, hh, qq: (bb, 0, hh))
        e1_spec = pl.BlockSpec((1, 1, 2, tq), lambda bb, hh, qq: (bb, hh, 0, q_blk0 + qq))
    kern = functools.partial(_flash_kernel, mode=mode, tq=tq, tk=tk, q_off=q_off, lam_init=lam_init)
    return pl.pallas_call(
        kern,
        grid=(b, n_col, t_q // tq),
        in_specs=[pl.BlockSpec((1, tq, LANES), lambda bb, hh, qq: (bb, qq, hh)),
                  pl.BlockSpec((1, t_k, LANES), lambda bb, hh, qq: (bb, 0, hh)),
                  kx_spec,
                  pl.BlockSpec((1, nt, LANES, tk), lambda bb, hh, qq: (bb, 0, hh, 0)),
                  mb_spec, e1_spec, const2(e2.shape), const2(e3.shape)],
        out_specs=pl.BlockSpec((1, tq, LANES), lambda bb, hh, qq: (bb, qq, hh)),
        out_shape=jax.ShapeDtypeStruct(q.shape, BF16),
        scratch_shapes=[pltpu.VMEM((2, 2 * LANES, tq), BF16),
                        pltpu.VMEM((2, 1, tq), F32),
                        pltpu.VMEM((2, LANES + BF16_ROWS, tq), F32),
                        pltpu.VMEM((2, tk, tq), F32), pltpu.VMEM((2, tk, tq), F32),
                        pltpu.VMEM((2, tk, tq), BF16), pltpu.VMEM((2, tk, tq), BF16),
                        pltpu.VMEM((2, 1, tq), F32), pltpu.VMEM((2, 1, tq), F32),
                        pltpu.VMEM((2, 1, tq), F32), pltpu.VMEM((2, 1, tq), F32)],
        compiler_params=_params(3),
        name="flash_" + mode,
    )(q, k, kx, vt, mask_bias, e1, e2, e3)


def _alibi_key_cols(slopes, t_k):
    pos = jnp.arange(t_k, dtype=jnp.int32)
    a1 = slopes[:, None] * ((pos >> 7) << 7).astype(F32)[None, :]
    a2 = slopes[:, None] * (pos & 127).astype(F32)[None, :]
    zero = jnp.zeros_like(a1)
    one = jnp.ones_like(a1)
    cols = jnp.stack([a1, a2, zero, a1, a2, zero, one, one, one], axis=-1)
    return jnp.pad(cols, ((0, 0), (0, 0), (0, LANES - EXT_USED))).astype(BF16)


def _band_kernel(q_ref, *rest, nblk, rows, thr0):
    k_refs = rest[:nblk]
    v_refs = rest[nblk:2 * nblk]
    bias_ref = rest[2 * nblk]
    o_ref = rest[2 * nblk + 1]
    c = pl.program_id(1)
    lk = nblk * rows
    valid = lax.broadcasted_iota(jnp.int32, (1, lk), 1) >= (thr0 - c * rows)
    lo = lax.broadcasted_iota(jnp.int32, (1, LANES), 1) < HEAD_DIM
    for pair in range(B_HEADS // 2):
        sl = slice(LANES * pair, LANES * pair + LANES)
        q = q_ref[0, :, sl]
        zero = jnp.zeros_like(q)
        k = jnp.concatenate([r[0, :, sl] for r in k_refs], axis=0).astype(BF16)
        v = jnp.concatenate([r[0, :, sl] for r in v_refs], axis=0).astype(BF16)
        outs = []
        for half in (0, 1):
            qh = jnp.where(lo, q, zero) if half == 0 else jnp.where(lo, zero, q)
            s = lax.dot_general(qh, k, (((1,), (1,)), ((), ())), preferred_element_type=F32)
            s = jnp.where(valid, s + bias_ref[2 * pair + half], NEG_INF)
            m = jnp.max(s, axis=1, keepdims=True)
            pr = jnp.exp(s - m)
            den = jnp.sum(pr, axis=1, keepdims=True)
            outs.append(jnp.dot(pr.astype(BF16), v, preferred_element_type=F32) * (1.0 / den))
        o_ref[0, :, sl] = jnp.where(lo, outs[0], outs[1]).astype(o_ref.dtype)


def _band_keys(rows):
    return -(-(B_WIN + rows) // LANES) * LANES


def _band_bias(table, rows):
    lk = _band_keys(rows)
    w = rows + lk - 1
    d = np.concatenate([np.arange(lk), np.arange(-(rows - 1), 0)])
    idx = np.clip(B_WIN - d, -B_REL_CLIP, B_REL_CLIP) + B_REL_CLIP
    e = table.astype(F32)[:, idx]
    toep = jnp.tile(e, (1, rows))[:, :rows * (w - 1)].reshape(-1, rows, w - 1)[:, :, :lk]
    i = np.arange(rows)[:, None]
    j = np.arange(lk)[None, :]
    lo_edge = (i // CHUNK) * CHUNK
    inband = (j >= lo_edge) & (j < lo_edge + B_WIN + CHUNK)
    return jnp.where(jnp.asarray(inband)[None], toep, NEG_INF)


def _band(q, k, v, table, *, rows, off):
    b, t, w = q.shape
    nblk = _band_keys(rows) // rows
    shift = (B_WIN - off) // rows
    last = k.shape[1] // rows - 1
    bias = _band_bias(table, rows)
    kv_specs = [pl.BlockSpec((1, rows, w), lambda bb, cc, i=i: (bb, jnp.clip(cc + i - shift, 0, last), 0))
                for i in range(nblk)]
    kern = functools.partial(_band_kernel, nblk=nblk, rows=rows, thr0=B_WIN - off)
    return pl.pallas_call(
        kern,
        grid=(b, t // rows),
        in_specs=[pl.BlockSpec((1, rows, w), lambda bb, cc: (bb, cc, 0))] + kv_specs + kv_specs
                 + [pl.BlockSpec(bias.shape, lambda bb, cc: (0, 0, 0))],
        out_specs=pl.BlockSpec((1, rows, w), lambda bb, cc: (bb, cc, 0)),
        out_shape=jax.ShapeDtypeStruct(q.shape, BF16),
        compiler_params=_params(2),
        name="band_attn",
    )(q, *([k] * nblk), *([v] * nblk), bias)


def _gelu_tanh(x):
    return 0.5 * x * (1.0 + jnp.tanh(math.sqrt(2.0 / math.pi) * (x + 0.044715 * (x * x * x))))


def _rglru_kernel(dx_ref, dg_ref, buf_ref, h0_ref, cw_ref, cb_ref, wa_ref, ba_ref, wx_ref, bx_ref, lam_ref,
                  od_ref, nbuf_ref, hl_ref, tail_ref, hst_ref, a_s, b_s, h_s, *, tt):
    j = pl.program_id(1)
    nj = pl.num_programs(1)

    @pl.when(j == 0)
    def _():
        tail_ref[...] = jnp.zeros(tail_ref.shape, F32)
        tail_ref[5:8, :] = buf_ref[0]
        hst_ref[...] = h0_ref[0]

    x = dx_ref[0]
    xs = jnp.concatenate([tail_ref[...], x], axis=0)
    cw = cw_ref[...]
    u = cb_ref[...] + cw[3:4] * x
    for tap in range(D_CONV - 1):
        u = u + cw[tap:tap + 1] * xs[5 + tap:5 + tap + tt]
    tail_ref[...] = x[tt - 8:tt]

    ub = u.astype(BF16)
    r = _sigmoid(jnp.dot(ub, wa_ref[...], preferred_element_type=F32) + ba_ref[...])
    gate_i = _sigmoid(jnp.dot(ub, wx_ref[...], preferred_element_type=F32) + bx_ref[...])
    log_a = -RG_C * r * _softplus(-lam_ref[...])
    a_s[...] = jnp.exp(log_a)
    b_s[...] = jnp.sqrt(-_expm1(2.0 * log_a)) * (gate_i * u)

    def body(i, h):
        base = pl.multiple_of(i * 8, 8)
        for rr in range(8):
            h = a_s[pl.ds(base + rr, 1), :] * h + b_s[pl.ds(base + rr, 1), :]
            h_s[pl.ds(base + rr, 1), :] = h
        return h

    h_fin = lax.fori_loop(0, tt // 8, body, hst_ref[...])
    hst_ref[...] = h_fin
    od_ref[0] = (h_s[...] * _gelu_tanh(dg_ref[0])).astype(od_ref.dtype)

    @pl.when(j == nj - 1)
    def _():
        nbuf_ref[0] = tail_ref[8 - (D_CONV - 1):8, :]
        hl_ref[0] = h_fin


def _rglru(dx, dg, buf, h0, cw, cb, wa, ba, wx, bx, lam, *, tt):
    b, t, w = dx.shape
    const2 = lambda shape: pl.BlockSpec(shape, lambda bb, jj: (0, 0))
    return pl.pallas_call(
        functools.partial(_rglru_kernel, tt=tt),
        grid=(b, t // tt),
        in_specs=[pl.BlockSpec((1, tt, w), lambda bb, jj: (bb, jj, 0)),
                  pl.BlockSpec((1, tt, w), lambda bb, jj: (bb, jj, 0)),
                  pl.BlockSpec((1, D_CONV - 1, w), lambda bb, jj: (bb, 0, 0)),
                  pl.BlockSpec((1, 1, w), lambda bb, jj: (bb, 0, 0)),
                  const2((D_CONV, w)), const2((1, w)), const2((w, w)), const2((1, w)),
                  const2((w, w)), const2((1, w)), const2((1, w))],
        out_specs=[pl.BlockSpec((1, tt, w), lambda bb, jj: (bb, jj, 0)),
                   pl.BlockSpec((1, D_CONV - 1, w), lambda bb, jj: (bb, 0, 0)),
                   pl.BlockSpec((1, 1, w), lambda bb, jj: (bb, 0, 0))],
        out_shape=[jax.ShapeDtypeStruct((b, t, w), BF16),
                   jax.ShapeDtypeStruct((b, D_CONV - 1, w), F32),
                   jax.ShapeDtypeStruct((b, 1, w), F32)],
        scratch_shapes=[pltpu.VMEM((8, w), F32), pltpu.VMEM((1, w), F32),
                        pltpu.VMEM((tt, w), F32), pltpu.VMEM((tt, w), F32), pltpu.VMEM((tt, w), F32)],
        compiler_params=_params(2),
        name="rglru",
    )(dx, dg, buf, h0, cw, cb, wa, ba, wx, bx, lam)


def _tail_kernel(x_ref, o1_ref, o2_ref, wo_ref, g_ref, w1_ref, w3_ref, w2_ref, fg_ref, out_ref,
                 y_ref, h_ref, acc_ref, *, final_norm):
    j = pl.program_id(1)
    half = wo_ref.shape[0] // 2

    @pl.when(j == 0)
    def _():
        y = (x_ref[...]
             + jnp.dot(o1_ref[...], wo_ref[0:half, :], preferred_element_type=F32)
             + jnp.dot(o2_ref[...], wo_ref[half:, :], preferred_element_type=F32))
        y_ref[...] = y
        h_ref[...] = _rms(y, g_ref[...]).astype(BF16)
        acc_ref[...] = jnp.zeros(acc_ref.shape, F32)

    h = h_ref[...]
    a = jnp.dot(h, w1_ref[...], preferred_element_type=F32)
    g = jnp.dot(h, w3_ref[...], preferred_element_type=F32)
    act = (a * _sigmoid(a) * g).astype(BF16)
    acc_ref[...] += jnp.dot(act, w2_ref[...], preferred_element_type=F32)

    @pl.when(j == pl.num_programs(1) - 1)
    def _():
        y = y_ref[...] + acc_ref[...]
        if final_norm:
            y = _rms(y, fg_ref[...])
        out_ref[...] = y


def _tail(x, o1, o2, wo, g, w1, w3, w2, fg, *, final_norm, tm, hc):
    n = x.shape[0]
    hw = o1.shape[1]
    return pl.pallas_call(
        functools.partial(_tail_kernel, final_norm=final_norm),
        grid=(n // tm, FFN_HIDDEN // hc),
        in_specs=[pl.BlockSpec((tm, D_MODEL), lambda i, j: (i, 0)),
                  pl.BlockSpec((tm, hw), lambda i, j: (i, 0)),
                  pl.BlockSpec((tm, hw), lambda i, j: (i, 0)),
                  pl.BlockSpec(wo.shape, lambda i, j: (0, 0)),
                  pl.BlockSpec((1, D_MODEL), lambda i, j: (0, 0)),
                  pl.BlockSpec((D_MODEL, hc), lambda i, j: (0, j)),
                  pl.BlockSpec((D_MODEL, hc), lambda i, j: (0, j)),
                  pl.BlockSpec((hc, D_MODEL), lambda i, j: (j, 0)),
                  pl.BlockSpec((1, D_MODEL), lambda i, j: (0, 0))],
        out_specs=pl.BlockSpec((tm, D_MODEL), lambda i, j: (i, 0)),
        out_shape=jax.ShapeDtypeStruct((n, D_MODEL), F32),
        scratch_shapes=[pltpu.VMEM((tm, D_MODEL), F32), pltpu.VMEM((tm, D_MODEL), BF16),
                        pltpu.VMEM((tm, D_MODEL), F32)],
        compiler_params=_params(2, 56),
        name="tail_ffn",
    )(x, o1, o2, wo, g, w1, w3, w2, fg)


def _pad_time(x, front, back):
    return jnp.pad(x, ((0, 0), (front, back), (0, 0)))


def _block_diag(w):
    n, d, _ = w.shape
    out = jnp.zeros((n * d, n * d), w.dtype)
    for i in range(n):
        out = out.at[i * d:(i + 1) * d, i * d:(i + 1) * d].set(w[i])
    return out


def _pad_queries(q, tq):
    return _pad_time(q, 0, (-q.shape[1]) % tq)


def _vt_tiles(v, tile):
    b, t_k, w = v.shape
    return jnp.swapaxes(v.reshape(b, t_k // tile, tile, w), 2, 3).astype(BF16)


def _layer_ab(x, past, wts, *, tm, tq, tk, rows):
    b, t, _ = x.shape
    n = b * t
    w_in, slopes, a_lambda, a_subln_g, b_rel_bias, g_mix = wts
    prompt = past is None
    assert not prompt or tm == tk
    aq, ak, av, bq, bk, bv, *extra = _norm_proj(
        x.reshape(n, D_MODEL), g_mix, w_in, jnp.zeros((1, LANES), F32),
        [512] * 6, [BF16, F32, F32, BF16, F32, F32], [SCALE, 1.0, 1.0, SCALE, 1.0, 1.0], False, tm,
        extras=((1, "heads"), (2, "heads")) + (((2, "tile_t"),) if prompt else ()))
    (ak_heads, av_heads), vts = extra[:2], extra[2:]
    r3 = lambda z: z.reshape(b, t, 512)
    aq, ak, av, bq, bk, bv = map(r3, (aq, ak, av, bq, bk, bv))
    if prompt:
        ka, q_off = ak, 0
        vta = vts[0].reshape(b, t // tm, 512, tm)
        kb, vb, off = bk, bv, 0
        keep = min(B_WIN, t)
        nbk, nbv = bk[:, t - keep:], bv[:, t - keep:]
    else:
        cak, cav, cbk, cbv = past
        p_len = cak.shape[1]
        back = (-(p_len + t)) % tk
        ka = _pad_time(jnp.concatenate([cak.reshape(b, p_len, 512), ak], axis=1), 0, back)
        vta = _vt_tiles(_pad_time(jnp.concatenate([cav.reshape(b, p_len, 512), av], axis=1), 0, back), tk)
        q_off = p_len
        lb = cbk.shape[1]
        kb = jnp.concatenate([cbk.reshape(b, lb, 512), bk], axis=1)
        vb = jnp.concatenate([cbv.reshape(b, lb, 512), bv], axis=1)
        off = lb
        nbk, nbv = kb[:, t:], vb[:, t:]
    o_a = _flash("diff", _pad_queries(aq, tq), ka, _alibi_key_cols(slopes, ka.shape[1]), vta,
                 slopes, a_lambda, a_subln_g.reshape(1, LANES),
                 tq=tq, tk=tk, q_off=q_off, lam_init=_lambda_init(0))[:, :t]
    o_b = _band(bq, kb, vb, b_rel_bias, rows=rows, off=off)
    state = (ak_heads.reshape(b, t, A_HEADS, 2 * HEAD_DIM), av_heads.reshape(b, t, A_HEADS, 2 * HEAD_DIM),
             nbk.reshape(b, -1, B_HEADS, HEAD_DIM), nbv.reshape(b, -1, B_HEADS, HEAD_DIM))
    return o_a.reshape(n, 512), o_b.reshape(n, 512), state


def _layer_cd(x, past, wts, *, tm, tq, tk, tt, tc):
    b, t, _ = x.shape
    n = b * t
    w_in, fb, cw, cb, wa, ba, wx, bx, lam, g_mix = wts
    prompt = past is None
    assert not prompt or tm == tk
    cq, ck, cv, dx, dg, logf, *extra = _norm_proj(
        x.reshape(n, D_MODEL), g_mix, w_in, fb,
        [512] * 5 + [LANES], [BF16, F32, F32, F32, F32, F32], [SCALE, 1.0, 1.0, 1.0, 1.0, 1.0], True, tm,
        extras=((2, "tile_t"), (1, "chan"), (2, "chan")) if prompt else (), seq=t)
    r3 = lambda z: z.reshape(b, t, z.shape[-1])
    cq, ck, cv, dx, dg, logf = map(r3, (cq, ck, cv, dx, dg, logf))
    to_state = lambda z: z.reshape(b, t, C_HEADS, HEAD_DIM)
    ck_state, cv_state = to_state(ck), to_state(cv)
    if prompt:
        kc, lf_all, q_off = ck, logf, 0
        vtc = extra[0].reshape(b, t // tm, 512, tm)
        ck_state, cv_state = (jnp.transpose(z.reshape(b, C_HEADS, HEAD_DIM, t), (0, 3, 1, 2)) for z in extra[1:])
        buf = jnp.zeros((b, D_CONV - 1, D_WIDTH), F32)
        h0 = jnp.zeros((b, 1, D_WIDTH), F32)
    else:
        cck, ccv, cclogf, buf, h0 = past
        p_len = cck.shape[1]
        back = (-(p_len + t)) % tk
        kc = _pad_time(jnp.concatenate([cck.reshape(b, p_len, 512), ck], axis=1), 0, back)
        vtc = _vt_tiles(_pad_time(jnp.concatenate([ccv.reshape(b, p_len, 512), cv], axis=1), 0, back), tk)
        q_off = p_len
        lf_all = _pad_time(jnp.concatenate(
            [jnp.pad(cclogf.astype(F32), ((0, 0), (0, 0), (0, LANES - C_HEADS))), logf], axis=1), 0, back)
        h0 = h0.reshape(b, 1, D_WIDTH)
    kx, f_rows = _fox_prep(lf_all, tc)
    f_rows = f_rows.reshape(b, C_HEADS // 2, 2, -1)
    dummy = jnp.zeros((1, LANES), F32)
    o_c = _flash("fox", _pad_queries(cq, tq), kc, kx, vtc, f_rows, dummy, dummy,
                 tq=tq, tk=tk, q_off=q_off)[:, :t]
    o_d, nbuf, hl = _rglru(dx, dg, buf, h0, cw, cb, wa, ba, wx, bx, lam, tt=tt)
    state = (ck_state, cv_state, logf[:, :, :C_HEADS], nbuf, hl.reshape(b, D_WIDTH))
    return o_c.reshape(n, 512), o_d.reshape(n, 512), state


def kernel(x_prompt, x_sample, cache_a_k, cache_a_v, cache_b_k, cache_b_v, cache_c_k, cache_c_v, cache_c_logf, state_d_conv, state_d_h, norm_mix_g, norm_ffn_g, ab_w_in, ab_w_out, a_lambda, a_subln_g, b_rel_bias, cd_w_in, cd_w_out, c_f_bias, d_conv_w, d_conv_b, d_w_a, d_b_a, d_w_x, d_b_x, d_lambda, ffn_w1, ffn_w3, ffn_w2, final_g):
    slopes = jnp.asarray([2.0 ** (-8.0 * (h + 1) / A_HEADS) for h in range(A_HEADS)], F32)
    row = lambda z: z.reshape(1, -1).astype(F32)

    ab_wts = (ab_w_in.astype(BF16), slopes, a_lambda.astype(F32), a_subln_g.astype(F32), b_rel_bias,
              row(norm_mix_g[0]))
    w_cd = jnp.concatenate([cd_w_in[:, :1536], cd_w_in[:, 1544:], cd_w_in[:, 1536:1544],
                            jnp.zeros((D_MODEL, LANES - C_HEADS), cd_w_in.dtype)], axis=1).astype(BF16)
    fb = jnp.pad(c_f_bias.astype(F32), (0, LANES - C_HEADS)).reshape(1, LANES)
    cd_wts = (w_cd, fb, d_conv_w.astype(F32), row(d_conv_b), _block_diag(d_w_a).astype(BF16), row(d_b_a),
              _block_diag(d_w_x).astype(BF16), row(d_b_x), row(d_lambda), row(norm_mix_g[1]))
    wo = (ab_w_out.astype(BF16), cd_w_out.astype(BF16))
    w1, w3, w2 = ffn_w1.astype(BF16), ffn_w3.astype(BF16), ffn_w2.astype(BF16)
    fg = row(final_g)

    def trunk(x, past_ab, past_cd, cfg):
        b, t, _ = x.shape
        o_a, o_b, st_ab = _layer_ab(x, past_ab, ab_wts, tm=cfg["tm"], tq=cfg["tq"], tk=cfg["tk"],
                                    rows=cfg["rows"])
        y = _tail(x.reshape(b * t, D_MODEL), o_a, o_b, wo[0], row(norm_ffn_g[0]), w1[0], w3[0], w2[0], fg,
                  final_norm=False, tm=cfg["tm"], hc=cfg["hc"])
        o_c, o_d, st_cd = _layer_cd(y.reshape(b, t, D_MODEL), past_cd, cd_wts, tm=cfg["tm"], tq=cfg["tq"],
                                    tk=cfg["tk"], tt=cfg["tt"], tc=cfg["tc"])
        y = _tail(y, o_c, o_d, wo[1], row(norm_ffn_g[1]), w1[1], w3[1], w2[1], fg,
                  final_norm=True, tm=cfg["tm"], hc=cfg["hc"])
        return (y.reshape(b, t, D_MODEL),) + st_ab + st_cd

    prompt_cfg = dict(tm=512, tq=512, tk=512, rows=256, hc=1408, tt=1024, tc=512)
    sample_cfg = dict(tm=512, tq=128, tk=512, rows=64, hc=1408, tt=64, tc=512)
    outp = trunk(x_prompt, None, None, prompt_cfg)
    outs = trunk(x_sample, (cache_a_k, cache_a_v, cache_b_k, cache_b_v),
                 (cache_c_k, cache_c_v, cache_c_logf, state_d_conv, state_d_h), sample_cfg)
    return (outp[0], outs[0]) + outp[1:] + outs[1:]
```

```python
import functools
import math

import numpy as np
import jax
import jax.numpy as jnp
from jax import lax
from jax.experimental import pallas as pl
from jax.experimental.pallas import tpu as pltpu

F32 = jnp.float32
BF16 = jnp.bfloat16

D_MODEL = 1024
CHUNK = 64
HEAD_DIM = 64
EPS = 1e-6
NEG_INF = -1e30
SCALE = HEAD_DIM ** -0.5
A_HEADS = 4
B_HEADS = 8
B_WIN = 512
B_REL_CLIP = 128
C_HEADS = 8
D_WIDTH = 512
D_CONV = 4
RG_C = 8.0
FFN_HIDDEN = 2816
LANES = 128
BF16_ROWS = 16
MIB = 1024 * 1024

EXT_ONES_K = 6
EXT_USED = 9


def _lambda_init(layer):
    return 0.8 - 0.6 * math.exp(-0.3 * layer)


def _params(n_axes, vmem_mib=48):
    return pltpu.CompilerParams(dimension_semantics=("arbitrary",) * n_axes,
                                vmem_limit_bytes=vmem_mib * MIB)


def _rms(x, g):
    return x * lax.rsqrt(jnp.mean(x * x, axis=-1, keepdims=True) + EPS) * g


def _sigmoid(x):
    return 1.0 / (1.0 + jnp.exp(-x))


def _expm1(x):
    u = jnp.exp(x)
    return jnp.where(u == 1.0, x, (u - 1.0) * x / jnp.log(jnp.where(u == 1.0, 2.0, u)))


def _softplus(x):
    return jnp.maximum(x, 0.0) + jnp.log1p(jnp.exp(-jnp.abs(x)))


def _split3(x):
    x1 = x.astype(BF16)
    r1 = x - x1.astype(F32)
    x2 = r1.astype(BF16)
    x3 = (r1 - x2.astype(F32)).astype(BF16)
    return x1, x2, x3


def _norm_proj_kernel(x_ref, g_ref, w_ref, fb_ref, *out_refs, widths, scales, logsig_last, extras):
    h = _rms(x_ref[...], g_ref[...]).astype(BF16)
    extra_refs = out_refs[len(widths):]
    off = 0
    for idx, (o_ref, wd) in enumerate(zip(out_refs, widths)):
        y = jnp.dot(h, w_ref[:, off:off + wd], preferred_element_type=F32)
        if logsig_last and idx == len(widths) - 1:
            y = -_softplus(-(y + fb_ref[...]))
        if scales[idx] != 1.0:
            y = y * scales[idx]
        o_ref[...] = y.astype(o_ref.dtype)
        kinds = {kind: r for (seg, kind), r in zip(extras, extra_refs) if seg == idx}
        if "tile_t" in kinds or "chan" in kinds:
            yt = y.T
            if "tile_t" in kinds:
                kinds["tile_t"][0] = yt.astype(BF16)
            if "chan" in kinds:
                kinds["chan"][0] = yt
        if "heads" in kinds:
            for head in range(wd // LANES):
                kinds["heads"][:, head, :] = y[:, LANES * head:LANES * head + LANES]
        off += wd


def _norm_proj(x, g, w, fb, widths, dtypes, scales, logsig_last, tm, extras=(), seq=None):
    n = x.shape[0]
    kern = functools.partial(_norm_proj_kernel, widths=tuple(widths), scales=tuple(scales),
                             logsig_last=logsig_last, extras=tuple(extras))
    extra_specs, extra_shapes = [], []
    for seg, kind in extras:
        wd = widths[seg]
        if kind == "tile_t":
            extra_specs.append(pl.BlockSpec((1, wd, tm), lambda i: (i, 0, 0)))
            extra_shapes.append(jax.ShapeDtypeStruct((n // tm, wd, tm), BF16))
        elif kind == "heads":
            extra_specs.append(pl.BlockSpec((tm, wd // LANES, LANES), lambda i: (i, 0, 0)))
            extra_shapes.append(jax.ShapeDtypeStruct((n, wd // LANES, LANES), F32))
        else:
            per = seq // tm
            extra_specs.append(pl.BlockSpec((1, wd, tm), lambda i, per=per: (i // per, 0, i % per)))
            extra_shapes.append(jax.ShapeDtypeStruct((n // seq, wd, seq), F32))
    return pl.pallas_call(
        kern,
        grid=(n // tm,),
        in_specs=[pl.BlockSpec((tm, D_MODEL), lambda i: (i, 0)),
                  pl.BlockSpec((1, D_MODEL), lambda i: (0, 0)),
                  pl.BlockSpec(w.shape, lambda i: (0, 0)),
                  pl.BlockSpec((1, LANES), lambda i: (0, 0))],
        out_specs=[pl.BlockSpec((tm, wd), lambda i: (i, 0)) for wd in widths] + extra_specs,
        out_shape=[jax.ShapeDtypeStruct((n, wd), dt) for wd, dt in zip(widths, dtypes)] + extra_shapes,
        compiler_params=_params(1),
        name="norm_proj",
    )(x, g, w, fb)


def _fox_prep_kernel(x_ref, sel_ref, ones_ref, kx_ref, ft_ref, carry_ref, *, tt):
    @pl.when(pl.program_id(1) == 0)
    def _():
        carry_ref[...] = jnp.zeros(carry_ref.shape, F32)

    lane = lax.broadcasted_iota(jnp.int32, (1, LANES), 1)
    x = jnp.where(lane < C_HEADS, x_ref[0], 0.0)
    row = lax.broadcasted_iota(jnp.int32, (tt, tt), 0)
    col = lax.broadcasted_iota(jnp.int32, (tt, tt), 1)
    tri = jnp.where(col <= row, 1.0, 0.0).astype(BF16)
    x1, x2, x3 = _split3(x)
    f = (jnp.dot(tri, x1, preferred_element_type=F32) + jnp.dot(tri, x2, preferred_element_type=F32)
         + jnp.dot(tri, x3, preferred_element_type=F32)) + carry_ref[0:1, :]
    carry_ref[0:1, :] = f[tt - 1:tt, :]
    f1, f2, f3 = _split3(f)
    pieces = jnp.concatenate([f1, f2, f3], axis=1)
    kx_ref[0] = (ones_ref[...] - jnp.dot(pieces, sel_ref[...], preferred_element_type=F32)).astype(BF16)
    r = lax.broadcasted_iota(jnp.int32, (BF16_ROWS, LANES), 0)
    c = lax.broadcasted_iota(jnp.int32, (BF16_ROWS, LANES), 1)
    eye = jnp.where(r == c, 1.0, 0.0).astype(BF16)
    dn = (((1,), (1,)), ((), ()))
    ft = (lax.dot_general(eye, f1, dn, preferred_element_type=F32)
          + lax.dot_general(eye, f2, dn, preferred_element_type=F32)
          + lax.dot_general(eye, f3, dn, preferred_element_type=F32))
    ft_ref[0] = ft[0:C_HEADS, :]


def _fox_sel():
    sel = np.zeros((3 * LANES, C_HEADS // 2 * LANES), np.float32)
    ones = np.zeros((1, C_HEADS // 2 * LANES), np.float32)
    for pair in range(C_HEADS // 2):
        for half in (0, 1):
            for piece in range(3):
                sel[piece * LANES + 2 * pair + half, pair * LANES + 3 * half + piece] = 1.0
        ones[0, pair * LANES + EXT_ONES_K:pair * LANES + EXT_USED] = 1.0
    return jnp.asarray(sel, BF16), jnp.asarray(ones, F32)


def _fox_prep(logf, tt):
    b, t, _ = logf.shape
    sel, ones = _fox_sel()
    return pl.pallas_call(
        functools.partial(_fox_prep_kernel, tt=tt),
        grid=(b, t // tt),
        in_specs=[pl.BlockSpec((1, tt, LANES), lambda i, j: (i, j, 0)),
                  pl.BlockSpec(sel.shape, lambda i, j: (0, 0)),
                  pl.BlockSpec(ones.shape, lambda i, j: (0, 0))],
        out_specs=[pl.BlockSpec((1, tt, sel.shape[1]), lambda i, j: (i, j, 0)),
                   pl.BlockSpec((1, C_HEADS, tt), lambda i, j: (i, 0, j))],
        out_shape=[jax.ShapeDtypeStruct((b, t, sel.shape[1]), BF16),
                   jax.ShapeDtypeStruct((b, C_HEADS, t), F32)],
        scratch_shapes=[pltpu.VMEM((8, LANES), F32)],
        compiler_params=_params(2),
        name="fox_prep",
    )(logf, sel, ones)


def _flash_kernel(q_ref, k_ref, kx_ref, vt_ref, mb_ref, e1_ref, e2_ref, e3_ref, o_ref,
                  qt_ref, m_ref, acc_ref, s0_ref, s1_ref, mx0_ref, mx1_ref, *, mode, tq, tk, q_off, lam_init):
    h = pl.program_id(1)
    q_start = q_off + pl.program_id(2) * tq
    row_lo = lax.broadcasted_iota(jnp.int32, (LANES, 1), 0) < HEAD_DIM

    qt = q_ref[0].astype(F32).T
    zero = jnp.zeros_like(qt)
    r16 = lax.broadcasted_iota(jnp.int32, (BF16_ROWS, 1), 0)
    for half in (0, 1):
        qt_ref[half, 0:LANES, :] = (jnp.where(row_lo, qt, zero) if half == 0
                                    else jnp.where(row_lo, zero, qt)).astype(BF16)
        if mode == "diff":
            qpos = q_start + lax.broadcasted_iota(jnp.int32, (1, tq), 1)
            b = -e1_ref[h] * qpos.astype(F32)
        else:
            b = e1_ref[0, 0][half:half + 1, :]
        b1, b2, b3 = (z.astype(F32) for z in _split3(b))
        own = (r16 >= 3 * half) & (r16 < 3 * half + 3)
        ext = jnp.where(own, 1.0, jnp.where(r16 == EXT_ONES_K, b1, jnp.where(
            r16 == EXT_ONES_K + 1, b2, jnp.where(r16 == EXT_ONES_K + 2, b3, 0.0))))
        qt_ref[half, LANES:LANES + BF16_ROWS, :] = ext.astype(BF16)
        qt_ref[half, LANES + BF16_ROWS:, :] = jnp.zeros((LANES - BF16_ROWS, tq), BF16)
    m_ref[...] = jnp.full(m_ref.shape, NEG_INF, F32)
    acc_ref[...] = jnp.zeros(acc_ref.shape, F32)

    def scores(t, buf):
        s_ref, mx_ref = buf
        ks = pl.multiple_of(t * tk, tk)
        kaug = jnp.concatenate([k_ref[0, pl.ds(ks, tk), :].astype(BF16), kx_ref[0, pl.ds(ks, tk), :]], axis=1)
        for half in (0, 1):
            s = jnp.dot(kaug, qt_ref[half], preferred_element_type=F32)
            s_ref[half] = s
            mx_ref[half] = jnp.max(s, axis=0, keepdims=True)

    def soft_values(t, buf, masked):
        s_ref, mx_ref = buf
        vt = jnp.concatenate([vt_ref[0, t], jnp.ones((BF16_ROWS, tk), BF16)], axis=0)
        for half in (0, 1):
            s = s_ref[half]
            if masked:
                s = s + mb_ref[0]
            m_prev = m_ref[half]
            m_new = jnp.maximum(m_prev, jnp.max(s, axis=0, keepdims=True) if masked else mx_ref[half])
            pr = jnp.exp(s - m_new).astype(BF16)
            acc_ref[half] = jnp.exp(m_prev - m_new) * acc_ref[half] + jnp.dot(vt, pr, preferred_element_type=F32)
            m_ref[half] = m_new

    n_int = lax.div(q_start, tk)
    n_oct = lax.div(n_int, 8)
    n_quads = lax.div(n_int - 8 * n_oct, 4)
    n_pairs = lax.div(n_int - 8 * n_oct - 4 * n_quads, 2)

    even = (s0_ref, mx0_ref)
    odd = (s1_ref, mx1_ref)
    bufs = (even, odd)

    def unrolled(count, base):
        def body(i, carry):
            t = base + count * i
            for j in range(count):
                scores(t + j + 1, bufs[(j + 1) % 2])
                soft_values(t + j, bufs[j % 2], False)
            return carry
        return body

    scores(0, even)
    lax.fori_loop(0, n_oct, unrolled(8, 0), 0)
    lax.fori_loop(0, n_quads, unrolled(4, 8 * n_oct), 0)
    lax.fori_loop(0, n_pairs, unrolled(2, 8 * n_oct + 4 * n_quads), 0)
    t0 = 8 * n_oct + 4 * n_quads + 2 * n_pairs

    @pl.when(n_int > t0)
    def _():
        scores(t0 + 1, odd)
        soft_values(t0, even, False)
        soft_values(t0 + 1, odd, True)

    @pl.when(n_int == t0)
    def _():
        soft_values(t0, even, True)

    a0 = acc_ref[0]
    a1 = acc_ref[1]
    o0 = a0[0:LANES] * (1.0 / a0[LANES:LANES + 1])
    o1 = a1[0:LANES] * (1.0 / a1[LANES:LANES + 1])
    if mode == "diff":
        a = e2_ref[...]
        lam = (jnp.exp(jnp.sum(a[0:1] * a[1:2], axis=1, keepdims=True))
               - jnp.exp(jnp.sum(a[2:3] * a[3:4], axis=1, keepdims=True)) + lam_init)
        o = (o0 - lam * o1).T
        o_ref[0] = (_rms(o, e3_ref[...]) * (1.0 - lam_init)).astype(o_ref.dtype)
    else:
        o_ref[0] = jnp.where(row_lo, o0, o1).T.astype(o_ref.dtype)


def _flash(mode, q, k, kx, vt, e1, e2, e3, *, tq, tk, q_off, lam_init=0.0):
    b, t_q, _ = q.shape
    t_k = k.shape[1]
    n_col = q.shape[2] // LANES
    nt = t_k // tk
    assert tk % tq == 0 and q_off % tq == 0 and t_k >= q_off + t_q and (tq == tk or t_q == tq)
    kpos = jnp.arange(tk, dtype=jnp.int32)[:, None]
    qpos = q_off % tk + jnp.arange(tq, dtype=jnp.int32)[None, :]
    if mode == "diff":
        corr = -2.0 * e1[:, None, None] * jnp.maximum(kpos - qpos, 0).astype(F32)[None]
        mask_bias = jnp.where(((kpos >> 6) <= (qpos >> 6))[None], corr, NEG_INF)
        mb_spec = pl.BlockSpec((1, tk, tq), lambda bb, hh, qq: (hh, 0, 0))
    else:
        mask_bias = jnp.where(kpos <= qpos, 0.0, NEG_INF).astype(F32)[None]
        mb_spec = pl.BlockSpec((1, tk, tq), lambda bb, hh, qq: (0, 0, 0))
    const2 = lambda shape: pl.BlockSpec(shape, lambda bb, hh, qq: (0, 0))
    if mode == "diff":
        kx_spec = pl.BlockSpec((1, t_k, LANES), lambda bb, hh, qq: (hh, 0, 0))
        e1_spec = pl.BlockSpec(memory_space=pltpu.SMEM)
    else:
        q_blk0 = q_off // tq
        kx_spec = pl.BlockSpec((1, t_k, LANES), lambda bb, hh, qq: (bb, 0, hh))
        e1_spec = pl.BlockSpec((1, 1, 2, tq), lambda bb, hh, qq: (bb, hh, 0, q_blk0 + qq))
    kern = functools.partial(_flash_kernel, mode=mode, tq=tq, tk=tk, q_off=q_off, lam_init=lam_init)
    return pl.pallas_call(
        kern,
        grid=(b, n_col, t_q // tq),
        in_specs=[pl.BlockSpec((1, tq, LANES), lambda bb, hh, qq: (bb, qq, hh)),
                  pl.BlockSpec((1, t_k, LANES), lambda bb, hh, qq: (bb, 0, hh)),
                  kx_spec,
                  pl.BlockSpec((1, nt, LANES, tk), lambda bb, hh, qq: (bb, 0, hh, 0)),
                  mb_spec, e1_spec, const2(e2.shape), const2(e3.shape)],
        out_specs=pl.BlockSpec((1, tq, LANES), lambda bb, hh, qq: (bb, qq, hh)),
        out_shape=jax.ShapeDtypeStruct(q.shape, BF16),
        scratch_shapes=[pltpu.VMEM((2, 2 * LANES, tq), BF16),
                        pltpu.VMEM((2, 1, tq), F32),
                        pltpu.VMEM((2, LANES + BF16_ROWS, tq), F32),
                        pltpu.VMEM((2, tk, tq), F32), pltpu.VMEM((2, tk, tq), F32),
                        pltpu.VMEM((2, 1, tq), F32), pltpu.VMEM((2, 1, tq), F32)],
        compiler_params=_params(3),
        name="flash_" + mode,
    )(q, k, kx, vt, mask_bias, e1, e2, e3)


def _alibi_key_cols(slopes, t_k):
    pos = jnp.arange(t_k, dtype=jnp.int32)
    a1 = slopes[:, None] * ((pos >> 7) << 7).astype(F32)[None, :]
    a2 = slopes[:, None] * (pos & 127).astype(F32)[None, :]
    zero = jnp.zeros_like(a1)
    one = jnp.ones_like(a1)
    cols = jnp.stack([a1, a2, zero, a1, a2, zero, one, one, one], axis=-1)
    return jnp.pad(cols, ((0, 0), (0, 0), (0, LANES - EXT_USED))).astype(BF16)


def _band_kernel(q_ref, *rest, nblk, rows, thr0):
    k_refs = rest[:nblk]
    v_refs = rest[nblk:2 * nblk]
    bias_ref = rest[2 * nblk]
    o_ref = rest[2 * nblk + 1]
    c = pl.program_id(1)
    lk = nblk * rows
    valid = lax.broadcasted_iota(jnp.int32, (1, lk), 1) >= (thr0 - c * rows)
    lo = lax.broadcasted_iota(jnp.int32, (1, LANES), 1) < HEAD_DIM
    for pair in range(B_HEADS // 2):
        sl = slice(LANES * pair, LANES * pair + LANES)
        q = q_ref[0, :, sl]
        zero = jnp.zeros_like(q)
        k = jnp.concatenate([r[0, :, sl] for r in k_refs], axis=0).astype(BF16)
        v = jnp.concatenate([r[0, :, sl] for r in v_refs], axis=0).astype(BF16)
        outs = []
        for half in (0, 1):
            qh = jnp.where(lo, q, zero) if half == 0 else jnp.where(lo, zero, q)
            s = lax.dot_general(qh, k, (((1,), (1,)), ((), ())), preferred_element_type=F32)
            s = jnp.where(valid, s + bias_ref[2 * pair + half], NEG_INF)
            m = jnp.max(s, axis=1, keepdims=True)
            pr = jnp.exp(s - m)
            den = jnp.sum(pr, axis=1, keepdims=True)
            outs.append(jnp.dot(pr.astype(BF16), v, preferred_element_type=F32) * (1.0 / den))
        o_ref[0, :, sl] = jnp.where(lo, outs[0], outs[1]).astype(o_ref.dtype)


def _band_keys(rows):
    return -(-(B_WIN + rows) // LANES) * LANES


def _band_bias(table, rows):
    lk = _band_keys(rows)
    w = rows + lk - 1
    d = np.concatenate([np.arange(lk), np.arange(-(rows - 1), 0)])
    idx = np.clip(B_WIN - d, -B_REL_CLIP, B_REL_CLIP) + B_REL_CLIP
    e = table.astype(F32)[:, idx]
    toep = jnp.tile(e, (1, rows))[:, :rows * (w - 1)].reshape(-1, rows, w - 1)[:, :, :lk]
    i = np.arange(rows)[:, None]
    j = np.arange(lk)[None, :]
    lo_edge = (i // CHUNK) * CHUNK
    inband = (j >= lo_edge) & (j < lo_edge + B_WIN + CHUNK)
    return jnp.where(jnp.asarray(inband)[None], toep, NEG_INF)


def _band(q, k, v, table, *, rows, off):
    b, t, w = q.shape
    nblk = _band_keys(rows) // rows
    shift = (B_WIN - off) // rows
    last = k.shape[1] // rows - 1
    bias = _band_bias(table, rows)
    kv_specs = [pl.BlockSpec((1, rows, w), lambda bb, cc, i=i: (bb, jnp.clip(cc + i - shift, 0, last), 0))
                for i in range(nblk)]
    kern = functools.partial(_band_kernel, nblk=nblk, rows=rows, thr0=B_WIN - off)
    return pl.pallas_call(
        kern,
        grid=(b, t // rows),
        in_specs=[pl.BlockSpec((1, rows, w), lambda bb, cc: (bb, cc, 0))] + kv_specs + kv_specs
                 + [pl.BlockSpec(bias.shape, lambda bb, cc: (0, 0, 0))],
        out_specs=pl.BlockSpec((1, rows, w), lambda bb, cc: (bb, cc, 0)),
        out_shape=jax.ShapeDtypeStruct(q.shape, BF16),
        compiler_params=_params(2),
        name="band_attn",
    )(q, *([k] * nblk), *([v] * nblk), bias)


def _gelu_tanh(x):
    return 0.5 * x * (1.0 + jnp.tanh(math.sqrt(2.0 / math.pi) * (x + 0.044715 * (x * x * x))))


def _rglru_kernel(dx_ref, dg_ref, buf_ref, h0_ref, cw_ref, cb_ref, wa_ref, ba_ref, wx_ref, bx_ref, lam_ref,
                  od_ref, nbuf_ref, hl_ref, tail_ref, hst_ref, a_s, b_s, h_s, *, tt):
    j = pl.program_id(1)
    nj = pl.num_programs(1)

    @pl.when(j == 0)
    def _():
        tail_ref[...] = jnp.zeros(tail_ref.shape, F32)
        tail_ref[5:8, :] = buf_ref[0]
        hst_ref[...] = h0_ref[0]

    x = dx_ref[0]
    xs = jnp.concatenate([tail_ref[...], x], axis=0)
    cw = cw_ref[...]
    u = cb_ref[...] + cw[3:4] * x
    for tap in range(D_CONV - 1):
        u = u + cw[tap:tap + 1] * xs[5 + tap:5 + tap + tt]
    tail_ref[...] = x[tt - 8:tt]

    ub = u.astype(BF16)
    r = _sigmoid(jnp.dot(ub, wa_ref[...], preferred_element_type=F32) + ba_ref[...])
    gate_i = _sigmoid(jnp.dot(ub, wx_ref[...], preferred_element_type=F32) + bx_ref[...])
    log_a = -RG_C * r * _softplus(-lam_ref[...])
    a_s[...] = jnp.exp(log_a)
    b_s[...] = jnp.sqrt(-_expm1(2.0 * log_a)) * (gate_i * u)

    def body(i, h):
        base = pl.multiple_of(i * 8, 8)
        for rr in range(8):
            h = a_s[pl.ds(base + rr, 1), :] * h + b_s[pl.ds(base + rr, 1), :]
            h_s[pl.ds(base + rr, 1), :] = h
        return h

    h_fin = lax.fori_loop(0, tt // 8, body, hst_ref[...])
    hst_ref[...] = h_fin
    od_ref[0] = (h_s[...] * _gelu_tanh(dg_ref[0])).astype(od_ref.dtype)

    @pl.when(j == nj - 1)
    def _():
        nbuf_ref[0] = tail_ref[8 - (D_CONV - 1):8, :]
        hl_ref[0] = h_fin


def _rglru(dx, dg, buf, h0, cw, cb, wa, ba, wx, bx, lam, *, tt):
    b, t, w = dx.shape
    const2 = lambda shape: pl.BlockSpec(shape, lambda bb, jj: (0, 0))
    return pl.pallas_call(
        functools.partial(_rglru_kernel, tt=tt),
        grid=(b, t // tt),
        in_specs=[pl.BlockSpec((1, tt, w), lambda bb, jj: (bb, jj, 0)),
                  pl.BlockSpec((1, tt, w), lambda bb, jj: (bb, jj, 0)),
                  pl.BlockSpec((1, D_CONV - 1, w), lambda bb, jj: (bb, 0, 0)),
                  pl.BlockSpec((1, 1, w), lambda bb, jj: (bb, 0, 0)),
                  const2((D_CONV, w)), const2((1, w)), const2((w, w)), const2((1, w)),
                  const2((w, w)), const2((1, w)), const2((1, w))],
        out_specs=[pl.BlockSpec((1, tt, w), lambda bb, jj: (bb, jj, 0)),
                   pl.BlockSpec((1, D_CONV - 1, w), lambda bb, jj: (bb, 0, 0)),
                   pl.BlockSpec((1, 1, w), lambda bb, jj: (bb, 0, 0))],
        out_shape=[jax.ShapeDtypeStruct((b, t, w), BF16),
                   jax.ShapeDtypeStruct((b, D_CONV - 1, w), F32),
                   jax.ShapeDtypeStruct((b, 1, w), F32)],
        scratch_shapes=[pltpu.VMEM((8, w), F32), pltpu.VMEM((1, w), F32),
                        pltpu.VMEM((tt, w), F32), pltpu.VMEM((tt, w), F32), pltpu.VMEM((tt, w), F32)],
        compiler_params=_params(2),
        name="rglru",
    )(dx, dg, buf, h0, cw, cb, wa, ba, wx, bx, lam)


def _tail_kernel(x_ref, o1_ref, o2_ref, wo_ref, g_ref, w1_ref, w3_ref, w2_ref, fg_ref, out_ref,
                 y_ref, h_ref, acc_ref, *, final_norm):
    j = pl.program_id(1)
    half = wo_ref.shape[0] // 2

    @pl.when(j == 0)
    def _():
        y = (x_ref[...]
             + jnp.dot(o1_ref[...], wo_ref[0:half, :], preferred_element_type=F32)
             + jnp.dot(o2_ref[...], wo_ref[half:, :], preferred_element_type=F32))
        y_ref[...] = y
        h_ref[...] = _rms(y, g_ref[...]).astype(BF16)
        acc_ref[...] = jnp.zeros(acc_ref.shape, F32)

    h = h_ref[...]
    a = jnp.dot(h, w1_ref[...], preferred_element_type=F32)
    g = jnp.dot(h, w3_ref[...], preferred_element_type=F32)
    act = (a * _sigmoid(a) * g).astype(BF16)
    acc_ref[...] += jnp.dot(act, w2_ref[...], preferred_element_type=F32)

    @pl.when(j == pl.num_programs(1) - 1)
    def _():
        y = y_ref[...] + acc_ref[...]
        if final_norm:
            y = _rms(y, fg_ref[...])
        out_ref[...] = y


def _tail(x, o1, o2, wo, g, w1, w3, w2, fg, *, final_norm, tm, hc):
    n = x.shape[0]
    hw = o1.shape[1]
    return pl.pallas_call(
        functools.partial(_tail_kernel, final_norm=final_norm),
        grid=(n // tm, FFN_HIDDEN // hc),
        in_specs=[pl.BlockSpec((tm, D_MODEL), lambda i, j: (i, 0)),
                  pl.BlockSpec((tm, hw), lambda i, j: (i, 0)),
                  pl.BlockSpec((tm, hw), lambda i, j: (i, 0)),
                  pl.BlockSpec(wo.shape, lambda i, j: (0, 0)),
                  pl.BlockSpec((1, D_MODEL), lambda i, j: (0, 0)),
                  pl.BlockSpec((D_MODEL, hc), lambda i, j: (0, j)),
                  pl.BlockSpec((D_MODEL, hc), lambda i, j: (0, j)),
                  pl.BlockSpec((hc, D_MODEL), lambda i, j: (j, 0)),
                  pl.BlockSpec((1, D_MODEL), lambda i, j: (0, 0))],
        out_specs=pl.BlockSpec((tm, D_MODEL), lambda i, j: (i, 0)),
        out_shape=jax.ShapeDtypeStruct((n, D_MODEL), F32),
        scratch_shapes=[pltpu.VMEM((tm, D_MODEL), F32), pltpu.VMEM((tm, D_MODEL), BF16),
                        pltpu.VMEM((tm, D_MODEL), F32)],
        compiler_params=_params(2, 56),
        name="tail_ffn",
    )(x, o1, o2, wo, g, w1, w3, w2, fg)


def _pad_time(x, front, back):
    return jnp.pad(x, ((0, 0), (front, back), (0, 0)))


def _block_diag(w):
    n, d, _ = w.shape
    out = jnp.zeros((n * d, n * d), w.dtype)
    for i in range(n):
        out = out.at[i * d:(i + 1) * d, i * d:(i + 1) * d].set(w[i])
    return out


def _pad_queries(q, tq):
    return _pad_time(q, 0, (-q.shape[1]) % tq)


def _vt_tiles(v, tile):
    b, t_k, w = v.shape
    return jnp.swapaxes(v.reshape(b, t_k // tile, tile, w), 2, 3).astype(BF16)


def _layer_ab(x, past, wts, *, tm, tq, tk, rows):
    b, t, _ = x.shape
    n = b * t
    w_in, slopes, a_lambda, a_subln_g, b_rel_bias, g_mix = wts
    prompt = past is None
    assert not prompt or tm == tk
    aq, ak, av, bq, bk, bv, *extra = _norm_proj(
        x.reshape(n, D_MODEL), g_mix, w_in, jnp.zeros((1, LANES), F32),
        [512] * 6, [BF16, F32, F32, BF16, F32, F32], [SCALE, 1.0, 1.0, SCALE, 1.0, 1.0], False, tm,
        extras=((1, "heads"), (2, "heads")) + (((2, "tile_t"),) if prompt else ()))
    (ak_heads, av_heads), vts = extra[:2], extra[2:]
    r3 = lambda z: z.reshape(b, t, 512)
    aq, ak, av, bq, bk, bv = map(r3, (aq, ak, av, bq, bk, bv))
    if prompt:
        ka, q_off = ak, 0
        vta = vts[0].reshape(b, t // tm, 512, tm)
        kb, vb, off = bk, bv, 0
        keep = min(B_WIN, t)
        nbk, nbv = bk[:, t - keep:], bv[:, t - keep:]
    else:
        cak, cav, cbk, cbv = past
        p_len = cak.shape[1]
        back = (-(p_len + t)) % tk
        ka = _pad_time(jnp.concatenate([cak.reshape(b, p_len, 512), ak], axis=1), 0, back)
        vta = _vt_tiles(_pad_time(jnp.concatenate([cav.reshape(b, p_len, 512), av], axis=1), 0, back), tk)
        q_off = p_len
        lb = cbk.shape[1]
        kb = jnp.concatenate([cbk.reshape(b, lb, 512), bk], axis=1)
        vb = jnp.concatenate([cbv.reshape(b, lb, 512), bv], axis=1)
        off = lb
        nbk, nbv = kb[:, t:], vb[:, t:]
    o_a = _flash("diff", _pad_queries(aq, tq), ka, _alibi_key_cols(slopes, ka.shape[1]), vta,
                 slopes, a_lambda, a_subln_g.reshape(1, LANES),
                 tq=tq, tk=tk, q_off=q_off, lam_init=_lambda_init(0))[:, :t]
    o_b = _band(bq, kb, vb, b_rel_bias, rows=rows, off=off)
    state = (ak_heads.reshape(b, t, A_HEADS, 2 * HEAD_DIM), av_heads.reshape(b, t, A_HEADS, 2 * HEAD_DIM),
             nbk.reshape(b, -1, B_HEADS, HEAD_DIM), nbv.reshape(b, -1, B_HEADS, HEAD_DIM))
    return o_a.reshape(n, 512), o_b.reshape(n, 512), state


def _layer_cd(x, past, wts, *, tm, tq, tk, tt, tc):
    b, t, _ = x.shape
    n = b * t
    w_in, fb, cw, cb, wa, ba, wx, bx, lam, g_mix = wts
    prompt = past is None
    assert not prompt or tm == tk
    cq, ck, cv, dx, dg, logf, *extra = _norm_proj(
        x.reshape(n, D_MODEL), g_mix, w_in, fb,
        [512] * 5 + [LANES], [BF16, F32, F32, F32, F32, F32], [SCALE, 1.0, 1.0, 1.0, 1.0, 1.0], True, tm,
        extras=((2, "tile_t"), (1, "chan"), (2, "chan")) if prompt else (), seq=t)
    r3 = lambda z: z.reshape(b, t, z.shape[-1])
    cq, ck, cv, dx, dg, logf = map(r3, (cq, ck, cv, dx, dg, logf))
    to_state = lambda z: z.reshape(b, t, C_HEADS, HEAD_DIM)
    ck_state, cv_state = to_state(ck), to_state(cv)
    if prompt:
        kc, lf_all, q_off = ck, logf, 0
        vtc = extra[0].reshape(b, t // tm, 512, tm)
        ck_state, cv_state = (jnp.transpose(z.reshape(b, C_HEADS, HEAD_DIM, t), (0, 3, 1, 2)) for z in extra[1:])
        buf = jnp.zeros((b, D_CONV - 1, D_WIDTH), F32)
        h0 = jnp.zeros((b, 1, D_WIDTH), F32)
    else:
        cck, ccv, cclogf, buf, h0 = past
        p_len = cck.shape[1]
        back = (-(p_len + t)) % tk
        kc = _pad_time(jnp.concatenate([cck.reshape(b, p_len, 512), ck], axis=1), 0, back)
        vtc = _vt_tiles(_pad_time(jnp.concatenate([ccv.reshape(b, p_len, 512), cv], axis=1), 0, back), tk)
        q_off = p_len
        lf_all = _pad_time(jnp.concatenate(
            [jnp.pad(cclogf.astype(F32), ((0, 0), (0, 0), (0, LANES - C_HEADS))), logf], axis=1), 0, back)
        h0 = h0.reshape(b, 1, D_WIDTH)
    kx, f_rows = _fox_prep(lf_all, tc)
    f_rows = f_rows.reshape(b, C_HEADS // 2, 2, -1)
    dummy = jnp.zeros((1, LANES), F32)
    o_c = _flash("fox", _pad_queries(cq, tq), kc, kx, vtc, f_rows, dummy, dummy,
                 tq=tq, tk=tk, q_off=q_off)[:, :t]
    o_d, nbuf, hl = _rglru(dx, dg, buf, h0, cw, cb, wa, ba, wx, bx, lam, tt=tt)
    state = (ck_state, cv_state, logf[:, :, :C_HEADS], nbuf, hl.reshape(b, D_WIDTH))
    return o_c.reshape(n, 512), o_d.reshape(n, 512), state


def kernel(x_prompt, x_sample, cache_a_k, cache_a_v, cache_b_k, cache_b_v, cache_c_k, cache_c_v, cache_c_logf, state_d_conv, state_d_h, norm_mix_g, norm_ffn_g, ab_w_in, ab_w_out, a_lambda, a_subln_g, b_rel_bias, cd_w_in, cd_w_out, c_f_bias, d_conv_w, d_conv_b, d_w_a, d_b_a, d_w_x, d_b_x, d_lambda, ffn_w1, ffn_w3, ffn_w2, final_g):
    slopes = jnp.asarray([2.0 ** (-8.0 * (h + 1) / A_HEADS) for h in range(A_HEADS)], F32)
    row = lambda z: z.reshape(1, -1).astype(F32)

    ab_wts = (ab_w_in.astype(BF16), slopes, a_lambda.astype(F32), a_subln_g.astype(F32), b_rel_bias,
              row(norm_mix_g[0]))
    w_cd = jnp.concatenate([cd_w_in[:, :1536], cd_w_in[:, 1544:], cd_w_in[:, 1536:1544],
                            jnp.zeros((D_MODEL, LANES - C_HEADS), cd_w_in.dtype)], axis=1).astype(BF16)
    fb = jnp.pad(c_f_bias.astype(F32), (0, LANES - C_HEADS)).reshape(1, LANES)
    cd_wts = (w_cd, fb, d_conv_w.astype(F32), row(d_conv_b), _block_diag(d_w_a).astype(BF16), row(d_b_a),
              _block_diag(d_w_x).astype(BF16), row(d_b_x), row(d_lambda), row(norm_mix_g[1]))
    wo = (ab_w_out.astype(BF16), cd_w_out.astype(BF16))
    w1, w3, w2 = ffn_w1.astype(BF16), ffn_w3.astype(BF16), ffn_w2.astype(BF16)
    fg = row(final_g)

    def trunk(x, past_ab, past_cd, cfg):
        b, t, _ = x.shape
        o_a, o_b, st_ab = _layer_ab(x, past_ab, ab_wts, tm=cfg["tm"], tq=cfg["tq"], tk=cfg["tk"],
                                    rows=cfg["rows"])
        y = _tail(x.reshape(b * t, D_MODEL), o_a, o_b, wo[0], row(norm_ffn_g[0]), w1[0], w3[0], w2[0], fg,
                  final_norm=False, tm=cfg["tm"], hc=cfg["hc"])
        o_c, o_d, st_cd = _layer_cd(y.reshape(b, t, D_MODEL), past_cd, cd_wts, tm=cfg["tm"], tq=cfg["tq"],
                                    tk=cfg["tk"], tt=cfg["tt"], tc=cfg["tc"])
        y = _tail(y, o_c, o_d, wo[1], row(norm_ffn_g[1]), w1[1], w3[1], w2[1], fg,
                  final_norm=True, tm=cfg["tm"], hc=cfg["hc"])
        return (y.reshape(b, t, D_MODEL),) + st_ab + st_cd

    prompt_cfg = dict(tm=512, tq=512, tk=512, rows=256, hc=1408, tt=1024, tc=512)
    sample_cfg = dict(tm=512, tq=128, tk=512, rows=64, hc=1408, tt=64, tc=512)
    outp = trunk(x_prompt, None, None, prompt_cfg)
    outs = trunk(x_sample, (cache_a_k, cache_a_v, cache_b_k, cache_b_v),
                 (cache_c_k, cache_c_v, cache_c_logf, state_d_conv, state_d_h), sample_cfg)
    return (outp[0], outs[0]) + outp[1:] + outs[1:]
```

```python
import functools
import math

import numpy as np
import jax
import jax.numpy as jnp
from jax import lax
from jax.experimental import pallas as pl
from jax.experimental.pallas import tpu as pltpu

F32 = jnp.float32
BF16 = jnp.bfloat16

D_MODEL = 1024
CHUNK = 64
HEAD_DIM = 64
EPS = 1e-6
NEG_INF = -1e30
SCALE = HEAD_DIM ** -0.5
A_HEADS = 4
B_HEADS = 8
B_WIN = 512
B_REL_CLIP = 128
C_HEADS = 8
D_WIDTH = 512
D_CONV = 4
RG_C = 8.0
FFN_HIDDEN = 2816
LANES = 128
BF16_ROWS = 16
MIB = 1024 * 1024

EXT_ONES_K = 6
EXT_USED = 9


def _lambda_init(layer):
    return 0.8 - 0.6 * math.exp(-0.3 * layer)


def _params(n_axes, vmem_mib=48):
    return pltpu.CompilerParams(dimension_semantics=("arbitrary",) * n_axes,
                                vmem_limit_bytes=vmem_mib * MIB)


def _rms(x, g):
    return x * lax.rsqrt(jnp.mean(x * x, axis=-1, keepdims=True) + EPS) * g


def _sigmoid(x):
    return 0.5 * jnp.tanh(0.5 * x) + 0.5


def _expm1(x):
    u = jnp.exp(x)
    return jnp.where(u == 1.0, x, (u - 1.0) * x / jnp.log(jnp.where(u == 1.0, 2.0, u)))


def _softplus(x):
    return jnp.maximum(x, 0.0) + jnp.log1p(jnp.exp(-jnp.abs(x)))


def _split3(x):
    x1 = x.astype(BF16)
    r1 = x - x1.astype(F32)
    x2 = r1.astype(BF16)
    x3 = (r1 - x2.astype(F32)).astype(BF16)
    return x1, x2, x3


def _norm_proj_kernel(x_ref, g_ref, w_ref, fb_ref, *out_refs, widths, scales, logsig_last, extras):
    h = _rms(x_ref[...], g_ref[...]).astype(BF16)
    extra_refs = out_refs[len(widths):]
    off = 0
    for idx, (o_ref, wd) in enumerate(zip(out_refs, widths)):
        y = jnp.dot(h, w_ref[:, off:off + wd], preferred_element_type=F32)
        if logsig_last and idx == len(widths) - 1:
            y = -_softplus(-(y + fb_ref[...]))
        if scales[idx] != 1.0:
            y = y * scales[idx]
        o_ref[...] = y.astype(o_ref.dtype)
        kinds = {kind: r for (seg, kind), r in zip(extras, extra_refs) if seg == idx}
        if "tile_t" in kinds or "chan" in kinds:
            yt = y.T
            if "tile_t" in kinds:
                kinds["tile_t"][0] = yt.astype(BF16)
            if "chan" in kinds:
                kinds["chan"][0] = yt
        if "heads" in kinds:
            for head in range(wd // LANES):
                kinds["heads"][:, head, :] = y[:, LANES * head:LANES * head + LANES]
        off += wd


def _norm_proj(x, g, w, fb, widths, dtypes, scales, logsig_last, tm, extras=(), seq=None):
    n = x.shape[0]
    kern = functools.partial(_norm_proj_kernel, widths=tuple(widths), scales=tuple(scales),
                             logsig_last=logsig_last, extras=tuple(extras))
    extra_specs, extra_shapes = [], []
    for seg, kind in extras:
        wd = widths[seg]
        if kind == "tile_t":
            extra_specs.append(pl.BlockSpec((1, wd, tm), lambda i: (i, 0, 0)))
            extra_shapes.append(jax.ShapeDtypeStruct((n // tm, wd, tm), BF16))
        elif kind == "heads":
            extra_specs.append(pl.BlockSpec((tm, wd // LANES, LANES), lambda i: (i, 0, 0)))
            extra_shapes.append(jax.ShapeDtypeStruct((n, wd // LANES, LANES), F32))
        else:
            per = seq // tm
            extra_specs.append(pl.BlockSpec((1, wd, tm), lambda i, per=per: (i // per, 0, i % per)))
            extra_shapes.append(jax.ShapeDtypeStruct((n // seq, wd, seq), F32))
    return pl.pallas_call(
        kern,
        grid=(n // tm,),
        in_specs=[pl.BlockSpec((tm, D_MODEL), lambda i: (i, 0)),
                  pl.BlockSpec((1, D_MODEL), lambda i: (0, 0)),
                  pl.BlockSpec(w.shape, lambda i: (0, 0)),
                  pl.BlockSpec((1, LANES), lambda i: (0, 0))],
        out_specs=[pl.BlockSpec((tm, wd), lambda i: (i, 0)) for wd in widths] + extra_specs,
        out_shape=[jax.ShapeDtypeStruct((n, wd), dt) for wd, dt in zip(widths, dtypes)] + extra_shapes,
        compiler_params=_params(1),
        name="norm_proj",
    )(x, g, w, fb)


def _fox_prep_kernel(x_ref, sel_ref, ones_ref, kx_ref, ft_ref, carry_ref, *, tt):
    @pl.when(pl.program_id(1) == 0)
    def _():
        carry_ref[...] = jnp.zeros(carry_ref.shape, F32)

    lane = lax.broadcasted_iota(jnp.int32, (1, LANES), 1)
    x = jnp.where(lane < C_HEADS, x_ref[0], 0.0)
    row = lax.broadcasted_iota(jnp.int32, (tt, tt), 0)
    col = lax.broadcasted_iota(jnp.int32, (tt, tt), 1)
    tri = jnp.where(col <= row, 1.0, 0.0).astype(BF16)
    x1, x2, x3 = _split3(x)
    f = (jnp.dot(tri, x1, preferred_element_type=F32) + jnp.dot(tri, x2, preferred_element_type=F32)
         + jnp.dot(tri, x3, preferred_element_type=F32)) + carry_ref[0:1, :]
    carry_ref[0:1, :] = f[tt - 1:tt, :]
    f1, f2, f3 = _split3(f)
    pieces = jnp.concatenate([f1, f2, f3], axis=1)
    kx_ref[0] = (ones_ref[...] - jnp.dot(pieces, sel_ref[...], preferred_element_type=F32)).astype(BF16)
    r = lax.broadcasted_iota(jnp.int32, (BF16_ROWS, LANES), 0)
    c = lax.broadcasted_iota(jnp.int32, (BF16_ROWS, LANES), 1)
    eye = jnp.where(r == c, 1.0, 0.0).astype(BF16)
    dn = (((1,), (1,)), ((), ()))
    ft = (lax.dot_general(eye, f1, dn, preferred_element_type=F32)
          + lax.dot_general(eye, f2, dn, preferred_element_type=F32)
          + lax.dot_general(eye, f3, dn, preferred_element_type=F32))
    ft_ref[0] = ft[0:C_HEADS, :]


def _fox_sel():
    sel = np.zeros((3 * LANES, C_HEADS // 2 * LANES), np.float32)
    ones = np.zeros((1, C_HEADS // 2 * LANES), np.float32)
    for pair in range(C_HEADS // 2):
        for half in (0, 1):
            for piece in range(3):
                sel[piece * LANES + 2 * pair + half, pair * LANES + 3 * half + piece] = 1.0
        ones[0, pair * LANES + EXT_ONES_K:pair * LANES + EXT_USED] = 1.0
    return jnp.asarray(sel, BF16), jnp.asarray(ones, F32)


def _fox_prep(logf, tt):
    b, t, _ = logf.shape
    sel, ones = _fox_sel()
    return pl.pallas_call(
        functools.partial(_fox_prep_kernel, tt=tt),
        grid=(b, t // tt),
        in_specs=[pl.BlockSpec((1, tt, LANES), lambda i, j: (i, j, 0)),
                  pl.BlockSpec(sel.shape, lambda i, j: (0, 0)),
                  pl.BlockSpec(ones.shape, lambda i, j: (0, 0))],
        out_specs=[pl.BlockSpec((1, tt, sel.shape[1]), lambda i, j: (i, j, 0)),
                   pl.BlockSpec((1, C_HEADS, tt), lambda i, j: (i, 0, j))],
        out_shape=[jax.ShapeDtypeStruct((b, t, sel.shape[1]), BF16),
                   jax.ShapeDtypeStruct((b, C_HEADS, t), F32)],
        scratch_shapes=[pltpu.VMEM((8, LANES), F32)],
        compiler_params=_params(2),
        name="fox_prep",
    )(logf, sel, ones)


def _flash_kernel(q_ref, k_ref, kx_ref, vt_ref, mb_ref, e1_ref, e2_ref, e3_ref, o_ref,
                  qt_ref, m_ref, acc_ref, s0_ref, s1_ref, mx0_ref, mx1_ref, *, mode, tq, tk, q_off, lam_init):
    h = pl.program_id(1)
    q_start = q_off + pl.program_id(2) * tq
    row_lo = lax.broadcasted_iota(jnp.int32, (LANES, 1), 0) < HEAD_DIM

    qt = q_ref[0].astype(F32).T
    zero = jnp.zeros_like(qt)
    r16 = lax.broadcasted_iota(jnp.int32, (BF16_ROWS, 1), 0)
    for half in (0, 1):
        qt_ref[half, 0:LANES, :] = (jnp.where(row_lo, qt, zero) if half == 0
                                    else jnp.where(row_lo, zero, qt)).astype(BF16)
        if mode == "diff":
            qpos = q_start + lax.broadcasted_iota(jnp.int32, (1, tq), 1)
            b = -e1_ref[h] * qpos.astype(F32)
        else:
            b = e1_ref[0, 0][half:half + 1, :]
        b1, b2, b3 = (z.astype(F32) for z in _split3(b))
        own = (r16 >= 3 * half) & (r16 < 3 * half + 3)
        ext = jnp.where(own, 1.0, jnp.where(r16 == EXT_ONES_K, b1, jnp.where(
            r16 == EXT_ONES_K + 1, b2, jnp.where(r16 == EXT_ONES_K + 2, b3, 0.0))))
        qt_ref[half, LANES:LANES + BF16_ROWS, :] = ext.astype(BF16)
        qt_ref[half, LANES + BF16_ROWS:, :] = jnp.zeros((LANES - BF16_ROWS, tq), BF16)
    m_ref[...] = jnp.full(m_ref.shape, NEG_INF, F32)
    acc_ref[...] = jnp.zeros(acc_ref.shape, F32)

    def scores(t, buf):
        s_ref, mx_ref = buf
        ks = pl.multiple_of(t * tk, tk)
        kaug = jnp.concatenate([k_ref[0, pl.ds(ks, tk), :].astype(BF16), kx_ref[0, pl.ds(ks, tk), :]], axis=1)
        for half in (0, 1):
            s = jnp.dot(kaug, qt_ref[half], preferred_element_type=F32)
            s_ref[half] = s
            mx_ref[half] = jnp.max(s, axis=0, keepdims=True)

    def soft_values(t, buf, masked):
        s_ref, mx_ref = buf
        vt = jnp.concatenate([vt_ref[0, t], jnp.ones((BF16_ROWS, tk), BF16)], axis=0)
        for half in (0, 1):
            s = s_ref[half]
            if masked:
                s = s + mb_ref[0]
            m_prev = m_ref[half]
            m_new = jnp.maximum(m_prev, jnp.max(s, axis=0, keepdims=True) if masked else mx_ref[half])
            pr = jnp.exp(s - m_new).astype(BF16)
            acc_ref[half] = jnp.exp(m_prev - m_new) * acc_ref[half] + jnp.dot(vt, pr, preferred_element_type=F32)
            m_ref[half] = m_new

    n_int = lax.div(q_start, tk)
    n_oct = lax.div(n_int, 8)
    n_quads = lax.div(n_int - 8 * n_oct, 4)
    n_pairs = lax.div(n_int - 8 * n_oct - 4 * n_quads, 2)

    even = (s0_ref, mx0_ref)
    odd = (s1_ref, mx1_ref)
    bufs = (even, odd)

    def unrolled(count, base):
        def body(i, carry):
            t = base + count * i
            for j in range(count):
                scores(t + j + 1, bufs[(j + 1) % 2])
                soft_values(t + j, bufs[j % 2], False)
            return carry
        return body

    scores(0, even)
    lax.fori_loop(0, n_oct, unrolled(8, 0), 0)
    lax.fori_loop(0, n_quads, unrolled(4, 8 * n_oct), 0)
    lax.fori_loop(0, n_pairs, unrolled(2, 8 * n_oct + 4 * n_quads), 0)
    t0 = 8 * n_oct + 4 * n_quads + 2 * n_pairs

    @pl.when(n_int > t0)
    def _():
        scores(t0 + 1, odd)
        soft_values(t0, even, False)
        soft_values(t0 + 1, odd, True)

    @pl.when(n_int == t0)
    def _():
        soft_values(t0, even, True)

    a0 = acc_ref[0]
    a1 = acc_ref[1]
    o0 = a0[0:LANES] * (1.0 / a0[LANES:LANES + 1])
    o1 = a1[0:LANES] * (1.0 / a1[LANES:LANES + 1])
    if mode == "diff":
        a = e2_ref[...]
        lam = (jnp.exp(jnp.sum(a[0:1] * a[1:2], axis=1, keepdims=True))
               - jnp.exp(jnp.sum(a[2:3] * a[3:4], axis=1, keepdims=True)) + lam_init)
        o = (o0 - lam * o1).T
        o_ref[0] = (_rms(o, e3_ref[...]) * (1.0 - lam_init)).astype(o_ref.dtype)
    else:
        o_ref[0] = jnp.where(row_lo, o0, o1).T.astype(o_ref.dtype)


def _flash(mode, q, k, kx, vt, e1, e2, e3, *, tq, tk, q_off, lam_init=0.0):
    b, t_q, _ = q.shape
    t_k = k.shape[1]
    n_col = q.shape[2] // LANES
    nt = t_k // tk
    assert tk % tq == 0 and q_off % tq == 0 and t_k >= q_off + t_q and (tq == tk or t_q == tq)
    kpos = jnp.arange(tk, dtype=jnp.int32)[:, None]
    qpos = q_off % tk + jnp.arange(tq, dtype=jnp.int32)[None, :]
    if mode == "diff":
        corr = -2.0 * e1[:, None, None] * jnp.maximum(kpos - qpos, 0).astype(F32)[None]
        mask_bias = jnp.where(((kpos >> 6) <= (qpos >> 6))[None], corr, NEG_INF)
        mb_spec = pl.BlockSpec((1, tk, tq), lambda bb, hh, qq: (hh, 0, 0))
    else:
        mask_bias = jnp.where(kpos <= qpos, 0.0, NEG_INF).astype(F32)[None]
        mb_spec = pl.BlockSpec((1, tk, tq), lambda bb, hh, qq: (0, 0, 0))
    const2 = lambda shape: pl.BlockSpec(shape, lambda bb, hh, qq: (0, 0))
    if mode == "diff":
        kx_spec = pl.BlockSpec((1, t_k, LANES), lambda bb, hh, qq: (hh, 0, 0))
        e1_spec = pl.BlockSpec(memory_space=pltpu.SMEM)
    else:
        q_blk0 = q_off // tq
        kx_spec = pl.BlockSpec((1, t_k, LANES), lambda bb, hh, qq: (bb, 0, hh))
        e1_spec = pl.BlockSpec((1, 1, 2, tq), lambda bb, hh, qq: (bb, hh, 0, q_blk0 + qq))
    kern = functools.partial(_flash_kernel, mode=mode, tq=tq, tk=tk, q_off=q_off, lam_init=lam_init)
    return pl.pallas_call(
        kern,
        grid=(b, n_col, t_q // tq),
        in_specs=[pl.BlockSpec((1, tq, LANES), lambda bb, hh, qq: (bb, qq, hh)),
                  pl.BlockSpec((1, t_k, LANES), lambda bb, hh, qq: (bb, 0, hh)),
                  kx_spec,
                  pl.BlockSpec((1, nt, LANES, tk), lambda bb, hh, qq: (bb, 0, hh, 0)),
                  mb_spec, e1_spec, const2(e2.shape), const2(e3.shape)],
        out_specs=pl.BlockSpec((1, tq, LANES), lambda bb, hh, qq: (bb, qq, hh)),
        out_shape=jax.ShapeDtypeStruct(q.shape, BF16),
        scratch_shapes=[pltpu.VMEM((2, 2 * LANES, tq), BF16),
                        pltpu.VMEM((2, 1, tq), F32),
                        pltpu.VMEM((2, LANES + BF16_ROWS, tq), F32),
                        pltpu.VMEM((2, tk, tq), F32), pltpu.VMEM((2, tk, tq), F32),
                        pltpu.VMEM((2, 1, tq), F32), pltpu.VMEM((2, 1, tq), F32)],
        compiler_params=_params(3),
        name="flash_" + mode,
    )(q, k, kx, vt, mask_bias, e1, e2, e3)


def _alibi_key_cols(slopes, t_k):
    pos = jnp.arange(t_k, dtype=jnp.int32)
    a1 = slopes[:, None] * ((pos >> 7) << 7).astype(F32)[None, :]
    a2 = slopes[:, None] * (pos & 127).astype(F32)[None, :]
    zero = jnp.zeros_like(a1)
    one = jnp.ones_like(a1)
    cols = jnp.stack([a1, a2, zero, a1, a2, zero, one, one, one], axis=-1)
    return jnp.pad(cols, ((0, 0), (0, 0), (0, LANES - EXT_USED))).astype(BF16)


def _band_kernel(q_ref, *rest, nblk, rows, thr0):
    k_refs = rest[:nblk]
    v_refs = rest[nblk:2 * nblk]
    bias_ref = rest[2 * nblk]
    o_ref = rest[2 * nblk + 1]
    c = pl.program_id(1)
    lk = nblk * rows
    valid = lax.broadcasted_iota(jnp.int32, (1, lk), 1) >= (thr0 - c * rows)
    lo = lax.broadcasted_iota(jnp.int32, (1, LANES), 1) < HEAD_DIM
    for pair in range(B_HEADS // 2):
        sl = slice(LANES * pair, LANES * pair + LANES)
        q = q_ref[0, :, sl]
        zero = jnp.zeros_like(q)
        k = jnp.concatenate([r[0, :, sl] for r in k_refs], axis=0).astype(BF16)
        v = jnp.concatenate([r[0, :, sl] for r in v_refs], axis=0).astype(BF16)
        outs = []
        for half in (0, 1):
            qh = jnp.where(lo, q, zero) if half == 0 else jnp.where(lo, zero, q)
            s = lax.dot_general(qh, k, (((1,), (1,)), ((), ())), preferred_element_type=F32)
            s = jnp.where(valid, s + bias_ref[2 * pair + half], NEG_INF)
            m = jnp.max(s, axis=1, keepdims=True)
            pr = jnp.exp(s - m)
            den = jnp.sum(pr, axis=1, keepdims=True)
            outs.append(jnp.dot(pr.astype(BF16), v, preferred_element_type=F32) * (1.0 / den))
        o_ref[0, :, sl] = jnp.where(lo, outs[0], outs[1]).astype(o_ref.dtype)


def _band_keys(rows):
    return -(-(B_WIN + rows) // LANES) * LANES


def _band_bias(table, rows):
    lk = _band_keys(rows)
    w = rows + lk - 1
    d = np.concatenate([np.arange(lk), np.arange(-(rows - 1), 0)])
    idx = np.clip(B_WIN - d, -B_REL_CLIP, B_REL_CLIP) + B_REL_CLIP
    e = table.astype(F32)[:, idx]
    toep = jnp.tile(e, (1, rows))[:, :rows * (w - 1)].reshape(-1, rows, w - 1)[:, :, :lk]
    i = np.arange(rows)[:, None]
    j = np.arange(lk)[None, :]
    lo_edge = (i // CHUNK) * CHUNK
    inband = (j >= lo_edge) & (j < lo_edge + B_WIN + CHUNK)
    return jnp.where(jnp.asarray(inband)[None], toep, NEG_INF)


def _band(q, k, v, table, *, rows, off):
    b, t, w = q.shape
    nblk = _band_keys(rows) // rows
    shift = (B_WIN - off) // rows
    last = k.shape[1] // rows - 1
    bias = _band_bias(table, rows)
    kv_specs = [pl.BlockSpec((1, rows, w), lambda bb, cc, i=i: (bb, jnp.clip(cc + i - shift, 0, last), 0))
                for i in range(nblk)]
    kern = functools.partial(_band_kernel, nblk=nblk, rows=rows, thr0=B_WIN - off)
    return pl.pallas_call(
        kern,
        grid=(b, t // rows),
        in_specs=[pl.BlockSpec((1, rows, w), lambda bb, cc: (bb, cc, 0))] + kv_specs + kv_specs
                 + [pl.BlockSpec(bias.shape, lambda bb, cc: (0, 0, 0))],
        out_specs=pl.BlockSpec((1, rows, w), lambda bb, cc: (bb, cc, 0)),
        out_shape=jax.ShapeDtypeStruct(q.shape, BF16),
        compiler_params=_params(2),
        name="band_attn",
    )(q, *([k] * nblk), *([v] * nblk), bias)


def _gelu_tanh(x):
    return 0.5 * x * (1.0 + jnp.tanh(math.sqrt(2.0 / math.pi) * (x + 0.044715 * (x * x * x))))


def _rglru_kernel(dx_ref, dg_ref, buf_ref, h0_ref, cw_ref, cb_ref, wa_ref, ba_ref, wx_ref, bx_ref, lam_ref,
                  od_ref, nbuf_ref, hl_ref, tail_ref, hst_ref, a_s, b_s, h_s, *, tt):
    j = pl.program_id(1)
    nj = pl.num_programs(1)

    @pl.when(j == 0)
    def _():
        tail_ref[...] = jnp.zeros(tail_ref.shape, F32)
        tail_ref[5:8, :] = buf_ref[0]
        hst_ref[...] = h0_ref[0]

    x = dx_ref[0]
    xs = jnp.concatenate([tail_ref[...], x], axis=0)
    cw = cw_ref[...]
    u = cb_ref[...] + cw[3:4] * x
    for tap in range(D_CONV - 1):
        u = u + cw[tap:tap + 1] * xs[5 + tap:5 + tap + tt]
    tail_ref[...] = x[tt - 8:tt]

    ub = u.astype(BF16)
    r = _sigmoid(jnp.dot(ub, wa_ref[...], preferred_element_type=F32) + ba_ref[...])
    gate_i = _sigmoid(jnp.dot(ub, wx_ref[...], preferred_element_type=F32) + bx_ref[...])
    log_a = -RG_C * r * _softplus(-lam_ref[...])
    a_s[...] = jnp.exp(log_a)
    b_s[...] = jnp.sqrt(-_expm1(2.0 * log_a)) * (gate_i * u)

    def body(i, h):
        base = pl.multiple_of(i * 8, 8)
        for rr in range(8):
            h = a_s[pl.ds(base + rr, 1), :] * h + b_s[pl.ds(base + rr, 1), :]
            h_s[pl.ds(base + rr, 1), :] = h
        return h

    h_fin = lax.fori_loop(0, tt // 8, body, hst_ref[...])
    hst_ref[...] = h_fin
    od_ref[0] = (h_s[...] * _gelu_tanh(dg_ref[0])).astype(od_ref.dtype)

    @pl.when(j == nj - 1)
    def _():
        nbuf_ref[0] = tail_ref[8 - (D_CONV - 1):8, :]
        hl_ref[0] = h_fin


def _rglru(dx, dg, buf, h0, cw, cb, wa, ba, wx, bx, lam, *, tt):
    b, t, w = dx.shape
    const2 = lambda shape: pl.BlockSpec(shape, lambda bb, jj: (0, 0))
    return pl.pallas_call(
        functools.partial(_rglru_kernel, tt=tt),
        grid=(b, t // tt),
        in_specs=[pl.BlockSpec((1, tt, w), lambda bb, jj: (bb, jj, 0)),
                  pl.BlockSpec((1, tt, w), lambda bb, jj: (bb, jj, 0)),
                  pl.BlockSpec((1, D_CONV - 1, w), lambda bb, jj: (bb, 0, 0)),
                  pl.BlockSpec((1, 1, w), lambda bb, jj: (bb, 0, 0)),
                  const2((D_CONV, w)), const2((1, w)), const2((w, w)), const2((1, w)),
                  const2((w, w)), const2((1, w)), const2((1, w))],
        out_specs=[pl.BlockSpec((1, tt, w), lambda bb, jj: (bb, jj, 0)),
                   pl.BlockSpec((1, D_CONV - 1, w), lambda bb, jj: (bb, 0, 0)),
                   pl.BlockSpec((1, 1, w), lambda bb, jj: (bb, 0, 0))],
        out_shape=[jax.ShapeDtypeStruct((b, t, w), BF16),
                   jax.ShapeDtypeStruct((b, D_CONV - 1, w), F32),
                   jax.ShapeDtypeStruct((b, 1, w), F32)],
        scratch_shapes=[pltpu.VMEM((8, w), F32), pltpu.VMEM((1, w), F32),
                        pltpu.VMEM((tt, w), F32), pltpu.VMEM((tt, w), F32), pltpu.VMEM((tt, w), F32)],
        compiler_params=_params(2),
        name="rglru",
    )(dx, dg, buf, h0, cw, cb, wa, ba, wx, bx, lam)


def _tail_kernel(x_ref, o1_ref, o2_ref, wo_ref, g_ref, w1_ref, w3_ref, w2_ref, fg_ref, out_ref,
                 y_ref, h_ref, acc_ref, *, final_norm):
    j = pl.program_id(1)
    half = wo_ref.shape[0] // 2

    @pl.when(j == 0)
    def _():
        y = (x_ref[...]
             + jnp.dot(o1_ref[...], wo_ref[0:half, :], preferred_element_type=F32)
             + jnp.dot(o2_ref[...], wo_ref[half:, :], preferred_element_type=F32))
        y_ref[...] = y
        h_ref[...] = _rms(y, g_ref[...]).astype(BF16)
        acc_ref[...] = jnp.zeros(acc_ref.shape, F32)

    h = h_ref[...]
    a = jnp.dot(h, w1_ref[...], preferred_element_type=F32)
    g = jnp.dot(h, w3_ref[...], preferred_element_type=F32)
    act = (a * _sigmoid(a) * g).astype(BF16)
    acc_ref[...] += jnp.dot(act, w2_ref[...], preferred_element_type=F32)

    @pl.when(j == pl.num_programs(1) - 1)
    def _():
        y = y_ref[...] + acc_ref[...]
        if final_norm:
            y = _rms(y, fg_ref[...])
        out_ref[...] = y


def _tail(x, o1, o2, wo, g, w1, w3, w2, fg, *, final_norm, tm, hc):
    n = x.shape[0]
    hw = o1.shape[1]
    return pl.pallas_call(
        functools.partial(_tail_kernel, final_norm=final_norm),
        grid=(n // tm, FFN_HIDDEN // hc),
        in_specs=[pl.BlockSpec((tm, D_MODEL), lambda i, j: (i, 0)),
                  pl.BlockSpec((tm, hw), lambda i, j: (i, 0)),
                  pl.BlockSpec((tm, hw), lambda i, j: (i, 0)),
                  pl.BlockSpec(wo.shape, lambda i, j: (0, 0)),
                  pl.BlockSpec((1, D_MODEL), lambda i, j: (0, 0)),
                  pl.BlockSpec((D_MODEL, hc), lambda i, j: (0, j)),
                  pl.BlockSpec((D_MODEL, hc), lambda i, j: (0, j)),
                  pl.BlockSpec((hc, D_MODEL), lambda i, j: (j, 0)),
                  pl.BlockSpec((1, D_MODEL), lambda i, j: (0, 0))],
        out_specs=pl.BlockSpec((tm, D_MODEL), lambda i, j: (i, 0)),
        out_shape=jax.ShapeDtypeStruct((n, D_MODEL), F32),
        scratch_shapes=[pltpu.VMEM((tm, D_MODEL), F32), pltpu.VMEM((tm, D_MODEL), BF16),
                        pltpu.VMEM((tm, D_MODEL), F32)],
        compiler_params=_params(2, 56),
        name="tail_ffn",
    )(x, o1, o2, wo, g, w1, w3, w2, fg)


def _pad_time(x, front, back):
    return jnp.pad(x, ((0, 0), (front, back), (0, 0)))


def _block_diag(w):
    n, d, _ = w.shape
    out = jnp.zeros((n * d, n * d), w.dtype)
    for i in range(n):
        out = out.at[i * d:(i + 1) * d, i * d:(i + 1) * d].set(w[i])
    return out


def _pad_queries(q, tq):
    return _pad_time(q, 0, (-q.shape[1]) % tq)


def _vt_tiles(v, tile):
    b, t_k, w = v.shape
    return jnp.swapaxes(v.reshape(b, t_k // tile, tile, w), 2, 3).astype(BF16)


def _layer_ab(x, past, wts, *, tm, tq, tk, rows):
    b, t, _ = x.shape
    n = b * t
    w_in, slopes, a_lambda, a_subln_g, b_rel_bias, g_mix = wts
    prompt = past is None
    assert not prompt or tm == tk
    aq, ak, av, bq, bk, bv, *extra = _norm_proj(
        x.reshape(n, D_MODEL), g_mix, w_in, jnp.zeros((1, LANES), F32),
        [512] * 6, [BF16, F32, F32, BF16, F32, F32], [SCALE, 1.0, 1.0, SCALE, 1.0, 1.0], False, tm,
        extras=((1, "heads"), (2, "heads")) + (((2, "tile_t"),) if prompt else ()))
    (ak_heads, av_heads), vts = extra[:2], extra[2:]
    r3 = lambda z: z.reshape(b, t, 512)
    aq, ak, av, bq, bk, bv = map(r3, (aq, ak, av, bq, bk, bv))
    if prompt:
        ka, q_off = ak, 0
        vta = vts[0].reshape(b, t // tm, 512, tm)
        kb, vb, off = bk, bv, 0
        keep = min(B_WIN, t)
        nbk, nbv = bk[:, t - keep:], bv[:, t - keep:]
    else:
        cak, cav, cbk, cbv = past
        p_len = cak.shape[1]
        back = (-(p_len + t)) % tk
        ka = _pad_time(jnp.concatenate([cak.reshape(b, p_len, 512), ak], axis=1), 0, back)
        vta = _vt_tiles(_pad_time(jnp.concatenate([cav.reshape(b, p_len, 512), av], axis=1), 0, back), tk)
        q_off = p_len
        lb = cbk.shape[1]
        kb = jnp.concatenate([cbk.reshape(b, lb, 512), bk], axis=1)
        vb = jnp.concatenate([cbv.reshape(b, lb, 512), bv], axis=1)
        off = lb
        nbk, nbv = kb[:, t:], vb[:, t:]
    o_a = _flash("diff", _pad_queries(aq, tq), ka, _alibi_key_cols(slopes, ka.shape[1]), vta,
                 slopes, a_lambda, a_subln_g.reshape(1, LANES),
                 tq=tq, tk=tk, q_off=q_off, lam_init=_lambda_init(0))[:, :t]
    o_b = _band(bq, kb, vb, b_rel_bias, rows=rows, off=off)
    state = (ak_heads.reshape(b, t, A_HEADS, 2 * HEAD_DIM), av_heads.reshape(b, t, A_HEADS, 2 * HEAD_DIM),
             nbk.reshape(b, -1, B_HEADS, HEAD_DIM), nbv.reshape(b, -1, B_HEADS, HEAD_DIM))
    return o_a.reshape(n, 512), o_b.reshape(n, 512), state


def _layer_cd(x, past, wts, *, tm, tq, tk, tt, tc):
    b, t, _ = x.shape
    n = b * t
    w_in, fb, cw, cb, wa, ba, wx, bx, lam, g_mix = wts
    prompt = past is None
    assert not prompt or tm == tk
    cq, ck, cv, dx, dg, logf, *extra = _norm_proj(
        x.reshape(n, D_MODEL), g_mix, w_in, fb,
        [512] * 5 + [LANES], [BF16, F32, F32, F32, F32, F32], [SCALE, 1.0, 1.0, 1.0, 1.0, 1.0], True, tm,
        extras=((2, "tile_t"), (1, "chan"), (2, "chan")) if prompt else (), seq=t)
    r3 = lambda z: z.reshape(b, t, z.shape[-1])
    cq, ck, cv, dx, dg, logf = map(r3, (cq, ck, cv, dx, dg, logf))
    to_state = lambda z: z.reshape(b, t, C_HEADS, HEAD_DIM)
    ck_state, cv_state = to_state(ck), to_state(cv)
    if prompt:
        kc, lf_all, q_off = ck, logf, 0
        vtc = extra[0].reshape(b, t // tm, 512, tm)
        ck_state, cv_state = (jnp.transpose(z.reshape(b, C_HEADS, HEAD_DIM, t), (0, 3, 1, 2)) for z in extra[1:])
        buf = jnp.zeros((b, D_CONV - 1, D_WIDTH), F32)
        h0 = jnp.zeros((b, 1, D_WIDTH), F32)
    else:
        cck, ccv, cclogf, buf, h0 = past
        p_len = cck.shape[1]
        back = (-(p_len + t)) % tk
        kc = _pad_time(jnp.concatenate([cck.reshape(b, p_len, 512), ck], axis=1), 0, back)
        vtc = _vt_tiles(_pad_time(jnp.concatenate([ccv.reshape(b, p_len, 512), cv], axis=1), 0, back), tk)
        q_off = p_len
        lf_all = _pad_time(jnp.concatenate(
            [jnp.pad(cclogf.astype(F32), ((0, 0), (0, 0), (0, LANES - C_HEADS))), logf], axis=1), 0, back)
        h0 = h0.reshape(b, 1, D_WIDTH)
    kx, f_rows = _fox_prep(lf_all, tc)
    f_rows = f_rows.reshape(b, C_HEADS // 2, 2, -1)
    dummy = jnp.zeros((1, LANES), F32)
    o_c = _flash("fox", _pad_queries(cq, tq), kc, kx, vtc, f_rows, dummy, dummy,
                 tq=tq, tk=tk, q_off=q_off)[:, :t]
    o_d, nbuf, hl = _rglru(dx, dg, buf, h0, cw, cb, wa, ba, wx, bx, lam, tt=tt)
    state = (ck_state, cv_state, logf[:, :, :C_HEADS], nbuf, hl.reshape(b, D_WIDTH))
    return o_c.reshape(n, 512), o_d.reshape(n, 512), state


def kernel(x_prompt, x_sample, cache_a_k, cache_a_v, cache_b_k, cache_b_v, cache_c_k, cache_c_v, cache_c_logf, state_d_conv, state_d_h, norm_mix_g, norm_ffn_g, ab_w_in, ab_w_out, a_lambda, a_subln_g, b_rel_bias, cd_w_in, cd_w_out, c_f_bias, d_conv_w, d_conv_b, d_w_a, d_b_a, d_w_x, d_b_x, d_lambda, ffn_w1, ffn_w3, ffn_w2, final_g):
    slopes = jnp.asarray([2.0 ** (-8.0 * (h + 1) / A_HEADS) for h in range(A_HEADS)], F32)
    row = lambda z: z.reshape(1, -1).astype(F32)

    ab_wts = (ab_w_in.astype(BF16), slopes, a_lambda.astype(F32), a_subln_g.astype(F32), b_rel_bias,
              row(norm_mix_g[0]))
    w_cd = jnp.concatenate([cd_w_in[:, :1536], cd_w_in[:, 1544:], cd_w_in[:, 1536:1544],
                            jnp.zeros((D_MODEL, LANES - C_HEADS), cd_w_in.dtype)], axis=1).astype(BF16)
    fb = jnp.pad(c_f_bias.astype(F32), (0, LANES - C_HEADS)).reshape(1, LANES)
    cd_wts = (w_cd, fb, d_conv_w.astype(F32), row(d_conv_b), _block_diag(d_w_a).astype(BF16), row(d_b_a),
              _block_diag(d_w_x).astype(BF16), row(d_b_x), row(d_lambda), row(norm_mix_g[1]))
    wo = (ab_w_out.astype(BF16), cd_w_out.astype(BF16))
    w1, w3, w2 = ffn_w1.astype(BF16), ffn_w3.astype(BF16), ffn_w2.astype(BF16)
    fg = row(final_g)

    def trunk(x, past_ab, past_cd, cfg):
        b, t, _ = x.shape
        o_a, o_b, st_ab = _layer_ab(x, past_ab, ab_wts, tm=cfg["tm"], tq=cfg["tq"], tk=cfg["tk"],
                                    rows=cfg["rows"])
        y = _tail(x.reshape(b * t, D_MODEL), o_a, o_b, wo[0], row(norm_ffn_g[0]), w1[0], w3[0], w2[0], fg,
                  final_norm=False, tm=cfg["tm"], hc=cfg["hc"])
        o_c, o_d, st_cd = _layer_cd(y.reshape(b, t, D_MODEL), past_cd, cd_wts, tm=cfg["tm"], tq=cfg["tq"],
                                    tk=cfg["tk"], tt=cfg["tt"], tc=cfg["tc"])
        y = _tail(y, o_c, o_d, wo[1], row(norm_ffn_g[1]), w1[1], w3[1], w2[1], fg,
                  final_norm=True, tm=cfg["tm"], hc=cfg["hc"])
        return (y.reshape(b, t, D_MODEL),) + st_ab + st_cd

    prompt_cfg = dict(tm=512, tq=512, tk=512, rows=256, hc=1408, tt=1024, tc=512)
    sample_cfg = dict(tm=512, tq=128, tk=512, rows=64, hc=1408, tt=64, tc=512)
    outp = trunk(x_prompt, None, None, prompt_cfg)
    outs = trunk(x_sample, (cache_a_k, cache_a_v, cache_b_k, cache_b_v),
                 (cache_c_k, cache_c_v, cache_c_logf, state_d_conv, state_d_h), sample_cfg)
    return (outp[0], outs[0]) + outp[1:] + outs[1:]
```
